```python
import math
import jax, jax.numpy as jnp
from jax import lax
import numpy as np

D_MODEL = 1024
BATCH = 4
SEQ = 8192
DEPTH = 1
DEC_BATCH = 128
DEC_SEQ = 1
PAST_LEN = 8192
PAGE_SIZE = 128

DIFF_HEADS = 4
DIFF_QK_DIM = 64
DIFF_V_DIM = 2 * DIFF_QK_DIM
DIFF_WIDTH = DIFF_HEADS * DIFF_V_DIM
DSA_HEADS = 4
DSA_HEAD_DIM = 128
DSA_WIDTH = DSA_HEADS * DSA_HEAD_DIM
D_MIX = DIFF_WIDTH + DSA_WIDTH
IDX_HEADS = 8
IDX_DIM = 64
TOPK_MAX = 256
PLE_DIM = 256
Q_BLOCK = 128
NORM_EPS = 1e-6

SPLIT_SIZES = (
    DIFF_HEADS * 2 * DIFF_QK_DIM,
    DIFF_HEADS * 2 * DIFF_QK_DIM,
    DIFF_WIDTH,
    DIFF_WIDTH,
    DSA_WIDTH,
    DSA_WIDTH,
    DSA_WIDTH,
    DSA_WIDTH,
    IDX_HEADS * IDX_DIM,
    IDX_DIM,
    IDX_HEADS,
)
SPLIT_POINTS = tuple(int(c) for c in np.cumsum(SPLIT_SIZES)[:-1])
IN_COLS = int(sum(SPLIT_SIZES))

kernel_name = "hymba_diffattn_dsa_decode_step"


def _rmsnorm(x, g):
    xf = x.astype(jnp.float32)
    y = xf * lax.rsqrt(jnp.mean(xf * xf, axis=-1, keepdims=True) + NORM_EPS)
    return (y * g.astype(jnp.float32)).astype(x.dtype)


def _alibi_slopes():
    n = DIFF_HEADS + DSA_HEADS
    s = jnp.asarray(2.0 ** (-8.0 * np.arange(1, n + 1) / n), dtype=jnp.float32)
    return s[0::2], s[1::2]


def _lambda_init(layer):
    return 0.8 - 0.6 * math.exp(-0.3 * layer)


def _diff_lambda(lam_params, lam_init):
    lp = lam_params.astype(jnp.float32)
    return jnp.exp(jnp.sum(lp[0] * lp[1])) - jnp.exp(jnp.sum(lp[2] * lp[3])) + lam_init


def _project(h, w):
    z = jnp.einsum('bsd,dc->bsc', h, w)
    dq, dk, dv, dg, sq, sk, sv, sg, iq, ik, iw = jnp.split(z, SPLIT_POINTS, axis=-1)
    B, S = h.shape[:2]
    return dict(
        dq=dq.reshape(B, S, DIFF_HEADS, 2, DIFF_QK_DIM),
        dk=dk.reshape(B, S, DIFF_HEADS, 2 * DIFF_QK_DIM),
        dv=dv.reshape(B, S, DIFF_HEADS, DIFF_V_DIM),
        dg=dg,
        sq=sq.reshape(B, S, DSA_HEADS, DSA_HEAD_DIM),
        sk=sk.reshape(B, S, DSA_HEADS, DSA_HEAD_DIM),
        sv=sv.reshape(B, S, DSA_HEADS, DSA_HEAD_DIM),
        sg=sg,
        iq=iq.reshape(B, S, IDX_HEADS, IDX_DIM),
        ik=ik,
        iw=iw,
    )


def _diff_scores(q, k, qpos, kpos, slopes):
    B, S = k.shape[:2]
    k12 = k.reshape(B, S, DIFF_HEADS, 2, DIFF_QK_DIM)
    s = jnp.einsum('bthmd,bshmd->mbhts', q, k12, preferred_element_type=jnp.float32) * (DIFF_QK_DIM ** -0.5)
    dist = (qpos[:, None] - kpos[None, :]).astype(jnp.float32)
    return s - slopes[:, None, None] * dist


def _diff_finish(o, subln, lam_init):
    B, T = o.shape[:2]
    return (_rmsnorm(o, subln) * (1.0 - lam_init)).reshape(B, T, DIFF_WIDTH)


def _diff_attn_prompt(q, k, v, lam, slopes):
    B, S = q.shape[:2]
    nb = S // Q_BLOCK
    kpos = jnp.arange(S)
    qb = jnp.moveaxis(q.reshape(B, nb, Q_BLOCK, DIFF_HEADS, 2, DIFF_QK_DIM), 1, 0)

    def block(args):
        qblk, i = args
        qpos = i * Q_BLOCK + jnp.arange(Q_BLOCK)
        s = _diff_scores(qblk, k, qpos, kpos, slopes)
        s = jnp.where(kpos[None, :] <= qpos[:, None], s, -jnp.inf)
        p = jax.nn.softmax(s, axis=-1)
        return jnp.einsum('bhts,bshd->bthd', p[0] - lam * p[1], v, preferred_element_type=jnp.float32)

    o = lax.map(block, (qb, jnp.arange(nb)))
    return jnp.moveaxis(o, 0, 1).reshape(B, S, DIFF_HEADS, DIFF_V_DIM)


def _online_update(carry, s, v):
    m, l, acc = carry
    m_new = jnp.maximum(m, jnp.max(s, axis=-1))
    alpha = jnp.exp(m - m_new)
    p = jnp.exp(s - m_new[..., None])
    l = l * alpha + jnp.sum(p, axis=-1)
    acc = acc * alpha[..., None] + jnp.einsum('mbhts,bshd->mbhtd', p, v, preferred_element_type=jnp.float32)
    return (m_new, l, acc)


def _diff_attn_sample(q, k_new, v_new, cache_k, cache_v, layer, page_table, lam, slopes):
    Bd, T = q.shape[:2]
    n_pages = page_table.shape[1]
    past = n_pages * PAGE_SIZE
    qpos = past + jnp.arange(T)
    init = (jnp.full((2, Bd, DIFF_HEADS, T), -jnp.inf, jnp.float32),
            jnp.zeros((2, Bd, DIFF_HEADS, T), jnp.float32),
            jnp.zeros((2, Bd, DIFF_HEADS, T, DIFF_V_DIM), jnp.float32))

    def body(carry, xs):
        pages, j = xs
        kpos = j * PAGE_SIZE + jnp.arange(PAGE_SIZE)
        s = _diff_scores(q, cache_k[layer, pages], qpos, kpos, slopes)
        return _online_update(carry, s, cache_v[layer, pages]), None

    carry, _ = lax.scan(body, init, (page_table.T, jnp.arange(n_pages)))
    s = _diff_scores(q, k_new, qpos, qpos, slopes)
    s = jnp.where(qpos[None, :] <= qpos[:, None], s, -jnp.inf)
    _, l, acc = _online_update(carry, s, v_new)
    o = acc / l[..., None]
    o = o[0] - lam * o[1]
    return jnp.transpose(o, (0, 2, 1, 3))


def _index_scores(iq, iw, ik):
    dot = jnp.einsum('bthd,bsd->bths', iq, ik, preferred_element_type=jnp.float32) * (IDX_DIM ** -0.5)
    w = iw.astype(jnp.float32) * (IDX_HEADS ** -0.5)
    return jnp.einsum('bth,bths->bts', w, jax.nn.relu(dot))


def _gather_rows(x, idx):
    return jax.vmap(lambda xb, ib: xb[ib])(x, idx)


def _gather_paged(pool, layer, page_table, idx):
    phys = jax.vmap(lambda pt, i: pt[i])(page_table, idx // PAGE_SIZE)
    return pool[layer, phys, idx % PAGE_SIZE]


def _sparse_attend(q, kg, vg, idx, qpos, slopes):
    dist = qpos[None, :, None] - idx
    s = jnp.einsum('bthd,btkhd->bhtk', q, kg, preferred_element_type=jnp.float32) * (DSA_HEAD_DIM ** -0.5)
    s = s - slopes[None, :, None, None] * dist[:, None].astype(jnp.float32)
    s = jnp.where((dist >= 0)[:, None], s, -jnp.inf)
    p = jax.nn.softmax(s, axis=-1)
    return jnp.einsum('bhtk,btkhd->bthd', p, vg, preferred_element_type=jnp.float32)


def _dsa_prompt(q, k, v, iq, ik, iw, slopes):
    B, S = q.shape[:2]
    nb = S // Q_BLOCK
    ksel = min(TOPK_MAX, S // 4)
    kpos = jnp.arange(S)

    def blocks(a):
        return jnp.moveaxis(a.reshape((B, nb, Q_BLOCK) + a.shape[2:]), 1, 0)

    def block(args):
        qblk, iqblk, iwblk, i = args
        qpos = i * Q_BLOCK + jnp.arange(Q_BLOCK)
        isc = _index_scores(iqblk, iwblk, ik)
        isc = jnp.where(kpos[None, None, :] <= qpos[None, :, None], isc, -jnp.inf)
        _, idx = lax.top_k(isc, ksel)
        return _sparse_attend(qblk, _gather_rows(k, idx), _gather_rows(v, idx), idx, qpos, slopes)

    o = lax.map(block, (blocks(q), blocks(iq), blocks(iw), jnp.arange(nb)))
    return jnp.moveaxis(o, 0, 1).reshape(B, S, DSA_WIDTH)


def _dsa_sample(q, k_new, v_new, iq, ik_new, iw, cache_k, cache_v, cache_ik, layer, page_table, slopes):
    Bd, T = q.shape[:2]
    n_pages = page_table.shape[1]
    past = n_pages * PAGE_SIZE
    ik_past = cache_ik[layer, page_table].reshape(Bd, past, IDX_DIM)
    ik_all = jnp.concatenate([ik_past, ik_new.astype(ik_past.dtype)], axis=1)
    L = past + T
    ksel = min(TOPK_MAX, L // 4)
    qpos = past + jnp.arange(T)
    kpos = jnp.arange(L)
    isc = _index_scores(iq, iw, ik_all)
    isc = jnp.where(kpos[None, None, :] <= qpos[None, :, None], isc, -jnp.inf)
    _, idx = lax.top_k(isc, ksel)
    in_past = (idx < past)[..., None, None]
    pidx = jnp.minimum(idx, past - 1)
    nidx = jnp.clip(idx - past, 0, T - 1)
    kg = jnp.where(in_past, _gather_paged(cache_k, layer, page_table, pidx), _gather_rows(k_new, nidx))
    vg = jnp.where(in_past, _gather_paged(cache_v, layer, page_table, pidx), _gather_rows(v_new, nidx))
    return _sparse_attend(q, kg, vg, idx, qpos, slopes).reshape(Bd, T, DSA_WIDTH)


def _residual_update(h, z, o_diff, o_dsa, p, w_out, g_ple, w_gate, w_proj):
    mix = jnp.concatenate([o_diff * jax.nn.silu(z['dg']), o_dsa * jax.nn.silu(z['sg'])], axis=-1).astype(h.dtype)
    h = h + jnp.einsum('bsc,cd->bsd', mix, w_out).astype(h.dtype)
    gate = jax.nn.sigmoid(jnp.einsum('bsd,de->bse', _rmsnorm(h, g_ple), w_gate).astype(jnp.float32))
    return h + (gate * jnp.einsum('bsk,kd->bsd', p, w_proj)).astype(h.dtype)


def setup_inputs(seed: int = 0) -> dict:
    key = jax.random.key(seed)
    ks = jax.random.split(key, 24)
    n_pages = PAST_LEN // PAGE_SIZE
    n_used = DEC_BATCH * n_pages
    n_pool = n_used + (n_used + 3) // 4
    page_table = jax.random.permutation(ks[0], n_pool)[:n_used].reshape(DEC_BATCH, n_pages).astype(jnp.int32)
    f = jnp.float32
    return {
        "x_prompt": jax.random.normal(ks[1], (BATCH, SEQ, D_MODEL), f),
        "x_sample": jax.random.normal(ks[2], (DEC_BATCH, DEC_SEQ, D_MODEL), f),
        "cache_diff_k": jax.random.normal(ks[3], (DEPTH, n_pool, PAGE_SIZE, DIFF_HEADS, 2 * DIFF_QK_DIM), f),
        "cache_diff_v": jax.random.normal(ks[4], (DEPTH, n_pool, PAGE_SIZE, DIFF_HEADS, DIFF_V_DIM), f),
        "cache_dsa_k": jax.random.normal(ks[5], (DEPTH, n_pool, PAGE_SIZE, DSA_HEADS, DSA_HEAD_DIM), f),
        "cache_dsa_v": jax.random.normal(ks[6], (DEPTH, n_pool, PAGE_SIZE, DSA_HEADS, DSA_HEAD_DIM), f),
        "cache_idx_k": jax.random.normal(ks[7], (DEPTH, n_pool, PAGE_SIZE, IDX_DIM), f),
        "page_table": page_table,
        "p_prompt": jax.random.normal(ks[8], (DEPTH, BATCH, SEQ, PLE_DIM), f),
        "p_sample": jax.random.normal(ks[9], (DEPTH, DEC_BATCH, DEC_SEQ, PLE_DIM), f),
        "norm_mix": 1.0 + 0.01 * jax.random.normal(ks[10], (DEPTH, D_MODEL), f),
        "w_in": jax.random.normal(ks[11], (DEPTH, D_MODEL, IN_COLS), f) * D_MODEL ** -0.5,
        "diff_lambda": 0.1 * jax.random.normal(ks[12], (DEPTH, 4, DIFF_QK_DIM), f),
        "diff_subln": 1.0 + 0.01 * jax.random.normal(ks[13], (DEPTH, DIFF_V_DIM), f),
        "w_out": jax.random.normal(ks[14], (DEPTH, D_MIX, D_MODEL), f) * D_MIX ** -0.5,
        "norm_ple": 1.0 + 0.01 * jax.random.normal(ks[15], (DEPTH, D_MODEL), f),
        "w_ple_gate": jax.random.normal(ks[16], (DEPTH, D_MODEL, D_MODEL), f) * D_MODEL ** -0.5,
        "w_ple_proj": jax.random.normal(ks[17], (DEPTH, PLE_DIM, D_MODEL), f) * PLE_DIM ** -0.5,
        "norm_final": 1.0 + 0.01 * jax.random.normal(ks[18], (D_MODEL,), f),
    }


def reference(x_prompt, x_sample, cache_diff_k, cache_diff_v, cache_dsa_k, cache_dsa_v, cache_idx_k,
              page_table, p_prompt, p_sample, norm_mix, w_in, diff_lambda, diff_subln, w_out,
              norm_ple, w_ple_gate, w_ple_proj, norm_final):
    slopes_diff, slopes_dsa = _alibi_slopes()
    hp, hs = x_prompt, x_sample
    dkp, dvp, skp, svp, ikp = [], [], [], [], []
    dks, dvs, sks, svs, iks = [], [], [], [], []
    for layer in range(DEPTH):
        lam_init = _lambda_init(layer)
        lam = _diff_lambda(diff_lambda[layer], lam_init)
        zp = _project(_rmsnorm(hp, norm_mix[layer]), w_in[layer])
        od = _diff_finish(_diff_attn_prompt(zp['dq'], zp['dk'], zp['dv'], lam, slopes_diff), diff_subln[layer], lam_init)
        osp = _dsa_prompt(zp['sq'], zp['sk'], zp['sv'], zp['iq'], zp['ik'], zp['iw'], slopes_dsa)
        hp = _residual_update(hp, zp, od, osp, p_prompt[layer], w_out[layer], norm_ple[layer], w_ple_gate[layer], w_ple_proj[layer])
        dkp.append(zp['dk']); dvp.append(zp['dv']); skp.append(zp['sk']); svp.append(zp['sv']); ikp.append(zp['ik'])
        zs = _project(_rmsnorm(hs, norm_mix[layer]), w_in[layer])
        od = _diff_finish(_diff_attn_sample(zs['dq'], zs['dk'], zs['dv'], cache_diff_k, cache_diff_v, layer,
                                            page_table, lam, slopes_diff), diff_subln[layer], lam_init)
        oss = _dsa_sample(zs['sq'], zs['sk'], zs['sv'], zs['iq'], zs['ik'], zs['iw'], cache_dsa_k, cache_dsa_v,
                          cache_idx_k, layer, page_table, slopes_dsa)
        hs = _residual_update(hs, zs, od, oss, p_sample[layer], w_out[layer], norm_ple[layer], w_ple_gate[layer], w_ple_proj[layer])
        dks.append(zs['dk']); dvs.append(zs['dv']); sks.append(zs['sk']); svs.append(zs['sv']); iks.append(zs['ik'])
    y_prompt = _rmsnorm(hp, norm_final)
    y_sample = _rmsnorm(hs, norm_final)
    return (y_prompt, y_sample,
            jnp.stack(dkp), jnp.stack(dvp), jnp.stack(skp), jnp.stack(svp), jnp.stack(ikp),
            jnp.stack(dks), jnp.stack(dvs), jnp.stack(sks), jnp.stack(svs), jnp.stack(iks))
```

```python
import functools
import math

import jax
import jax.numpy as jnp
import numpy as np
from jax import lax
from jax.experimental import pallas as pl
from jax.experimental.pallas import tpu as pltpu

F32 = jnp.float32
BF16 = jnp.bfloat16
I32 = jnp.int32

DIFF_HEADS = 4
DIFF_QK = 64
DSA_HEADS = 4
DSA_DIM = 128
IDX_HEADS = 8
IDX_DIM = 64
HEAD_W = 128
WIDTH = 512
TOPK_MAX = 256
NORM_EPS = 1e-6
SEG = 512
N_SEG = 9
TAIL_W = 256

NEG_BIG = -1e30
INT_MIN = -2 ** 31
INT_MAX = 2 ** 31 - 1
VMEM_LIMIT = 56 * 1024 * 1024
QROWS = 16


def _alibi_slopes():
    n = DIFF_HEADS + DSA_HEADS
    s = [2.0 ** (-8.0 * i / n) for i in range(1, n + 1)]
    return s[0::2], s[1::2]


SLOPES_DIFF, SLOPES_DSA = _alibi_slopes()
LAM_INIT = 0.8 - 0.6 * math.exp(-0.3 * 0)


def _nt_dot(a, b):
    return lax.dot_general(a, b, (((1,), (1,)), ((), ())), preferred_element_type=F32)


def _rms(x, g):
    ms = jnp.mean(x * x, axis=-1, keepdims=True)
    return x * lax.rsqrt(ms + NORM_EPS) * g


def _float_key(x):
    b = lax.bitcast_convert_type(x, I32)
    return b ^ (lax.shift_right_arithmetic(b, 31) & INT_MAX)


def _proj_kernel(x_ref, g_ref, w_ref, wt_ref,
                 dq_ref, dk32_ref, dk16_ref, dv32_ref, dv16_ref, dg_ref,
                 sq_ref, sk32_ref, sk16_ref, sv32_ref, sv16_ref, sg_ref,
                 iq_ref, ik32_ref, ikk_ref, iw_ref):
    xn = _rms(x_ref[...], g_ref[...]).astype(BF16)

    def seg(i):
        return jnp.dot(xn, w_ref[:, i * SEG:(i + 1) * SEG], preferred_element_type=F32)

    dq_ref[...] = (seg(0) * DIFF_QK ** -0.5).astype(BF16)
    z = seg(1); dk32_ref[...] = z; dk16_ref[...] = z.astype(BF16)
    z = seg(2); dv32_ref[...] = z; dv16_ref[...] = z.astype(BF16)
    dg_ref[...] = seg(3)
    sq_ref[...] = (seg(4) * DSA_DIM ** -0.5).astype(BF16)
    z = seg(5); sk32_ref[...] = z; sk16_ref[...] = z.astype(BF16)
    z = seg(6); sv32_ref[...] = z; sv16_ref[...] = z.astype(BF16)
    sg_ref[...] = seg(7)
    iq_ref[...] = (seg(8) * IDX_DIM ** -0.5).astype(BF16)
    zt = jnp.dot(xn, wt_ref[...], preferred_element_type=F32)
    ik32_ref[...] = zt[:, :IDX_DIM]
    ikk_ref[...] = zt[:, :2 * IDX_DIM].astype(BF16)
    iw_ref[...] = zt[:, 2 * IDX_DIM:2 * IDX_DIM + IDX_HEADS] * IDX_HEADS ** -0.5


def _project(x, g, w_main, w_tail, tm):
    m, d = x.shape
    row = lambda w, dt: jax.ShapeDtypeStruct((m, w), dt)
    rspec = lambda w: pl.BlockSpec((tm, w), lambda i: (i, 0))
    outs = [(WIDTH, BF16),
            (WIDTH, F32), (WIDTH, BF16), (WIDTH, F32), (WIDTH, BF16), (WIDTH, F32),
            (WIDTH, BF16),
            (WIDTH, F32), (WIDTH, BF16), (WIDTH, F32), (WIDTH, BF16), (WIDTH, F32),
            (WIDTH, BF16),
            (IDX_DIM, F32), (2 * IDX_DIM, BF16), (IDX_HEADS, F32)]
    return pl.pallas_call(
        _proj_kernel,
        grid=(m // tm,),
        in_specs=[rspec(d),
                  pl.BlockSpec((1, d), lambda i: (0, 0)),
                  pl.BlockSpec(w_main.shape, lambda i: (0, 0)),
                  pl.BlockSpec(w_tail.shape, lambda i: (0, 0))],
        out_specs=[rspec(w) for w, _ in outs],
        out_shape=[row(w, dt) for w, dt in outs],
        compiler_params=pltpu.CompilerParams(dimension_semantics=("arbitrary",),
                                             vmem_limit_bytes=VMEM_LIMIT),
        name="in_proj",
    )(x, g, w_main, w_tail)


def _diff_lambda(lp):
    a = jnp.exp(jnp.sum(lp[0:1] * lp[1:2], axis=-1, keepdims=True))
    b = jnp.exp(jnp.sum(lp[2:3] * lp[3:4], axis=-1, keepdims=True))
    return a - b + LAM_INIT


def _diff_finish(o2, lam, subln, t):
    od = o2[:t] - lam * o2[t:]
    return _rms(od, subln) * (1.0 - LAM_INIT)


def _diff_prompt_kernel(lam_ref, subln_ref, q_ref, k_ref, v_ref, o_ref, m_scr, l_scr, acc_scr, *, t):
    qi = pl.program_id(1)
    lam = _diff_lambda(lam_ref[...])
    row = lax.broadcasted_iota(I32, (2 * t, t), 0)
    col = lax.broadcasted_iota(I32, (2 * t, t), 1)
    row = jnp.where(row >= t, row - t, row)
    rel = (col - row).astype(F32)
    causal = col <= row
    lane = lax.broadcasted_iota(I32, (t, HEAD_W), 1)

    for h in range(DIFF_HEADS):
        slope = SLOPES_DIFF[h]
        hs = slice(h * HEAD_W, (h + 1) * HEAD_W)
        qh = q_ref[:, hs].astype(F32)
        qq = jnp.concatenate([jnp.where(lane < DIFF_QK, qh, 0.0), jnp.where(lane >= DIFF_QK, qh, 0.0)],
                             axis=0).astype(BF16)
        bias = slope * rel
        m_scr[...] = jnp.full(m_scr.shape, -jnp.inf, F32)
        l_scr[...] = jnp.zeros(l_scr.shape, F32)
        acc_scr[...] = jnp.zeros(acc_scr.shape, F32)

        def tile(j, masked):
            kt = k_ref[j][:, hs]
            vt = v_ref[j][:, hs]
            s = _nt_dot(qq, kt) + bias
            if masked:
                s = jnp.where(causal, s, NEG_BIG)
            off = slope * ((qi - j) * t).astype(F32)
            m_prev = m_scr[...]
            m_new = jnp.maximum(m_prev, jnp.max(s, axis=1, keepdims=True) - off)
            p = jnp.exp(s - (m_new[:, :1] + off))
            alpha = jnp.exp(m_prev - m_new)
            l_scr[...] = alpha * l_scr[...] + jnp.sum(p, axis=1, keepdims=True)
            acc_scr[...] = alpha * acc_scr[...] + jnp.dot(p.astype(BF16), vt, preferred_element_type=F32)
            m_scr[...] = m_new

        def body(j, c):
            tile(j, False)
            return c

        lax.fori_loop(0, qi, body, 0)
        tile(qi, True)
        o2 = acc_scr[...] / l_scr[...]
        o_ref[:, hs] = _diff_finish(o2, lam, subln_ref[...], t)


def _diff_prompt(lam_p, subln, dq, dk, dv, batch, seq, t):
    nt = seq // t
    k3 = dk.reshape(batch * nt, t, WIDTH)
    v3 = dv.reshape(batch * nt, t, WIDTH)
    kv_spec = pl.BlockSpec((nt, t, WIDTH), lambda b, i: (b, 0, 0), pipeline_mode=pl.Buffered(1))
    return pl.pallas_call(
        functools.partial(_diff_prompt_kernel, t=t),
        grid=(batch, nt),
        in_specs=[pl.BlockSpec(lam_p.shape, lambda b, i: (0, 0)),
                  pl.BlockSpec(subln.shape, lambda b, i: (0, 0)),
                  pl.BlockSpec((t, WIDTH), lambda b, i: (b * nt + i, 0)),
                  kv_spec, kv_spec],
        out_specs=pl.BlockSpec((t, WIDTH), lambda b, i: (b * nt + i, 0)),
        out_shape=jax.ShapeDtypeStruct((batch * seq, WIDTH), F32),
        scratch_shapes=[pltpu.VMEM((2 * t, HEAD_W), F32), pltpu.VMEM((2 * t, HEAD_W), F32),
                        pltpu.VMEM((2 * t, HEAD_W), F32)],
        compiler_params=pltpu.CompilerParams(dimension_semantics=("arbitrary", "arbitrary"),
                                             vmem_limit_bytes=VMEM_LIMIT),
        name="diff_prompt",
    )(lam_p, subln, dq, k3, v3)


def _select_threshold(key_ref, n_tiles, tk, rows, ksel, idx_bits, extra=None):
    chunks = tk // 128
    lane = lax.broadcasted_iota(I32, (rows, 128), 1)

    def count(pred):
        def body(j, cnt):
            for c in range(chunks):
                kt = key_ref[j, :, c * 128:(c + 1) * 128]
                cnt = cnt + pred(kt, lane + (j * tk + c * 128)).astype(I32)
            return cnt
        cnt = lax.fori_loop(0, n_tiles, body, jnp.zeros((rows, 128), I32))
        tot = jnp.sum(cnt, axis=1, keepdims=True)
        if extra is not None:
            tot = tot + pred(extra, n_tiles * tk).astype(I32)
        return tot

    def value_pass(i, thr):
        cand = thr + lax.shift_left(jnp.int32(1), 31 - i)
        n_ge = count(lambda kt, pos: kt >= cand[:, :kt.shape[1]])
        return jnp.where(n_ge >= ksel, cand, thr)

    thr = lax.fori_loop(0, 32, value_pass, jnp.full((rows, 128), INT_MIN, I32))
    n_gt = count(lambda kt, pos: kt > thr[:, :kt.shape[1]])
    n_eq = count(lambda kt, pos: kt == thr[:, :kt.shape[1]])
    need = ksel - n_gt

    def ties():
        def pos_pass(i, p):
            cand = p + lax.shift_left(jnp.int32(1), idx_bits - 1 - i)
            n_lt = count(lambda kt, pos: (kt == thr[:, :kt.shape[1]]) & (pos < cand[:, :kt.shape[1]]))
            return jnp.where(n_lt < need, cand, p)
        return lax.fori_loop(0, idx_bits, pos_pass, jnp.zeros((rows, 128), I32))

    def no_ties():
        return jnp.full((rows, 128), INT_MAX, I32)

    excess = jnp.max(n_gt + n_eq) > ksel
    return thr, lax.cond(excess, ties, no_ties)


def _selected(kt, pos, thr, last):
    return (kt > thr) | ((kt == thr) & (pos <= last))


def _dsa_prompt_kernel(q_ref, k_ref, v_ref, iq_ref, ikk_ref, iw_ref, o_ref,
                       key_scr, m_scr, l_scr, acc_scr, *, t, ksel, idx_bits):
    qi = pl.program_id(1)
    row = lax.broadcasted_iota(I32, (t, t), 0)
    col = lax.broadcasted_iota(I32, (t, t), 1)
    rel = (col - row).astype(F32)
    causal = col <= row
    lane = lax.broadcasted_iota(I32, (t, HEAD_W), 1)

    iqs, ws = [], []
    for h in range(IDX_HEADS):
        pair = iq_ref[:, (h // 2) * HEAD_W:(h // 2 + 1) * HEAD_W].astype(F32)
        keep = (lane < IDX_DIM) if h % 2 == 0 else (lane >= IDX_DIM)
        iqs.append(jnp.where(keep, pair, 0.0).astype(BF16))
        ws.append(jnp.broadcast_to(iw_ref[:, h:h + 1], (t, t)))

    def index_tile(j, masked):
        ikt = ikk_ref[j]
        acc = jnp.zeros((t, t), F32)
        for h in range(IDX_HEADS):
            acc = acc + ws[h] * jnp.maximum(_nt_dot(iqs[h], ikt), 0.0)
        if masked:
            acc = jnp.where(causal, acc, -jnp.inf)
        key_scr[j] = _float_key(acc)

    def index_body(j, c):
        index_tile(j, False)
        return c

    lax.fori_loop(0, qi, index_body, 0)
    index_tile(qi, True)

    thr, last = _select_threshold(key_scr, qi + 1, t, t, ksel, idx_bits)
    thr_t = jnp.tile(thr, (1, t // 128))
    last_t = jnp.tile(last, (1, t // 128))

    m_scr[...] = jnp.full(m_scr.shape, -jnp.inf, F32)
    l_scr[...] = jnp.zeros(l_scr.shape, F32)
    acc_scr[...] = jnp.zeros(acc_scr.shape, F32)
    biases = [SLOPES_DSA[h] * rel for h in range(DSA_HEADS)]

    def attend_tile(j, masked):
        sel = _selected(key_scr[j], col + j * t, thr_t, last_t)
        if masked:
            sel = sel & causal
        selbias = jnp.where(sel, 0.0, NEG_BIG)
        kt = k_ref[j]
        vt = v_ref[j]
        for h in range(DSA_HEADS):
            hs = slice(h * HEAD_W, (h + 1) * HEAD_W)
            s = _nt_dot(q_ref[:, hs], kt[:, hs]) + biases[h] + selbias
            off = SLOPES_DSA[h] * ((qi - j) * t).astype(F32)
            m_prev = m_scr[h]
            m_new = jnp.maximum(m_prev, jnp.max(s, axis=1, keepdims=True) - off)
            p = jnp.exp(s - (m_new[:, :1] + off))
            alpha = jnp.exp(m_prev - m_new)
            l_scr[h] = alpha * l_scr[h] + jnp.sum(p, axis=1, keepdims=True)
            acc_scr[h] = alpha * acc_scr[h] + jnp.dot(p.astype(BF16), vt[:, hs], preferred_element_type=F32)
            m_scr[h] = m_new

    def attend_body(j, c):
        attend_tile(j, False)
        return c

    lax.fori_loop(0, qi, attend_body, 0)
    attend_tile(qi, True)
    for h in range(DSA_HEADS):
        o_ref[:, h * HEAD_W:(h + 1) * HEAD_W] = acc_scr[h] / l_scr[h]


def _dsa_prompt(sq, sk, sv, iq, ikk, iw, batch, seq, t):
    nt = seq // t
    ksel = min(TOPK_MAX, seq // 4)
    assert t >= ksel and t % 128 == 0
    idx_bits = max(1, int(seq).bit_length())
    whole = lambda w: pl.BlockSpec((nt, t, w), lambda b, i: (b, 0, 0), pipeline_mode=pl.Buffered(1))
    rows = lambda w: pl.BlockSpec((t, w), lambda b, i: (b * nt + i, 0))
    return pl.pallas_call(
        functools.partial(_dsa_prompt_kernel, t=t, ksel=ksel, idx_bits=idx_bits),
        grid=(batch, nt),
        in_specs=[rows(WIDTH), whole(WIDTH), whole(WIDTH), rows(WIDTH), whole(2 * IDX_DIM), rows(IDX_HEADS)],
        out_specs=rows(WIDTH),
        out_shape=jax.ShapeDtypeStruct((batch * seq, WIDTH), F32),
        scratch_shapes=[pltpu.VMEM((nt, t, t), I32),
                        pltpu.VMEM((DSA_HEADS, t, HEAD_W), F32), pltpu.VMEM((DSA_HEADS, t, HEAD_W), F32),
                        pltpu.VMEM((DSA_HEADS, t, HEAD_W), F32)],
        compiler_params=pltpu.CompilerParams(dimension_semantics=("arbitrary", "arbitrary"),
                                             vmem_limit_bytes=VMEM_LIMIT),
        name="dsa_prompt",
    )(sq, sk.reshape(batch * nt, t, WIDTH), sv.reshape(batch * nt, t, WIDTH), iq,
      ikk.reshape(batch * nt, t, 2 * IDX_DIM), iw)


def _idx_sample_kernel(pt_ref, iq_ref, iw_ref, *refs, g):
    page_refs, o_ref = refs[:g], refs[g]
    pad = QROWS - IDX_HEADS
    iq = jnp.concatenate([iq_ref[0].astype(F32), jnp.zeros((pad, IDX_DIM), F32)], axis=0).astype(BF16)
    w = jnp.concatenate([iw_ref[0], jnp.zeros((pad, 1), F32)], axis=0)
    for i in range(g):
        keys = page_refs[i][0, 0].astype(BF16)
        d = _nt_dot(iq, keys)
        page = keys.shape[0]
        o_ref[0, :, i * page:(i + 1) * page] = jnp.sum(w * jnp.maximum(d, 0.0), axis=0, keepdims=True)


def _idx_sample(page_table, iq, iw, cache_ik, g):
    nb, n_pages = page_table.shape
    page = cache_ik.shape[2]

    def page_spec(i):
        return pl.BlockSpec((1, 1, page, IDX_DIM), lambda b, j, pt: (0, pt[b, j * g + i], 0, 0))

    return pl.pallas_call(
        functools.partial(_idx_sample_kernel, g=g),
        grid_spec=pltpu.PrefetchScalarGridSpec(
            num_scalar_prefetch=1,
            grid=(nb, n_pages // g),
            in_specs=[pl.BlockSpec((1, IDX_HEADS, IDX_DIM), lambda b, j, pt: (b, 0, 0)),
                      pl.BlockSpec((1, IDX_HEADS, 1), lambda b, j, pt: (b, 0, 0))]
                     + [page_spec(i) for i in range(g)],
            out_specs=pl.BlockSpec((1, 1, g * page), lambda b, j, pt: (b, 0, j)),
        ),
        out_shape=jax.ShapeDtypeStruct((nb, 1, n_pages * page), F32),
        compiler_params=pltpu.CompilerParams(dimension_semantics=("arbitrary", "arbitrary"),
                                             vmem_limit_bytes=VMEM_LIMIT),
        name="idx_sample",
    )(page_table, iq.reshape(nb, IDX_HEADS, IDX_DIM), iw.reshape(nb, IDX_HEADS, 1), *([cache_ik] * g))


def _select_sample_kernel(isc_ref, iq_ref, ikk_ref, iw_ref, bias_ref, new_ref, key_scr, *, tk, ksel, idx_bits):
    rows, past = isc_ref.shape
    nt = past // tk
    for j in range(nt):
        key_scr[j] = _float_key(isc_ref[:, j * tk:(j + 1) * tk] + 0.0)
    lane = lax.broadcasted_iota(I32, (rows, HEAD_W), 1)
    kk = ikk_ref[...].astype(F32)
    new = jnp.zeros((rows, 1), F32)
    for h in range(IDX_HEADS):
        pair = iq_ref[:, (h // 2) * HEAD_W:(h // 2 + 1) * HEAD_W].astype(F32)
        keep = (lane < IDX_DIM) if h % 2 == 0 else (lane >= IDX_DIM)
        d = jnp.sum(jnp.where(keep, pair * kk, 0.0), axis=1, keepdims=True)
        new = new + iw_ref[:, h:h + 1] * jnp.maximum(d, 0.0)
    new_key = _float_key(new)
    thr, last = _select_threshold(key_scr, nt, tk, rows, ksel, idx_bits, extra=new_key)
    lane_t = lax.broadcasted_iota(I32, (rows, tk), 1)
    thr_t = jnp.tile(thr, (1, tk // 128))
    last_t = jnp.tile(last, (1, tk // 128))
    for j in range(nt):
        sel = _selected(key_scr[j], lane_t + j * tk, thr_t, last_t)
        bias_ref[:, j * tk:(j + 1) * tk] = jnp.where(sel, 0.0, NEG_BIG)
    sel_new = _selected(new_key, past, thr, last)
    new_ref[...] = jnp.where(sel_new, 0.0, NEG_BIG)


def _select_sample(isc, iq, ikk, iw, tk):
    rows, past = isc.shape
    ksel = min(TOPK_MAX, (past + 1) // 4)
    idx_bits = int(past + 1).bit_length()
    full = lambda a: pl.BlockSpec(a.shape, lambda i: (0,) * a.ndim)
    return pl.pallas_call(
        functools.partial(_select_sample_kernel, tk=tk, ksel=ksel, idx_bits=idx_bits),
        grid=(1,),
        in_specs=[full(isc), full(iq), full(ikk), full(iw)],
        out_specs=[pl.BlockSpec((rows, past), lambda i: (0, 0)), pl.BlockSpec((rows, 128), lambda i: (0, 0))],
        out_shape=[jax.ShapeDtypeStruct((rows, past), F32), jax.ShapeDtypeStruct((rows, 128), F32)],
        scratch_shapes=[pltpu.VMEM((past // tk, rows, tk), I32)],
        compiler_params=pltpu.CompilerParams(dimension_semantics=("arbitrary",), vmem_limit_bytes=VMEM_LIMIT),
        name="select_sample",
    )(isc, iq, ikk, iw)


def _attn_sample_kernel(pt_ref, lam_ref, subln_ref, dq_ref, dkn_ref, dvn_ref, sq_ref, skn_ref, svn_ref,
                        bias_ref, new_ref, *refs, g, page):
    dk_refs, dv_refs, sk_refs, sv_refs = refs[:g], refs[g:2 * g], refs[2 * g:3 * g], refs[3 * g:4 * g]
    od_ref, os_ref, md_scr, ld_scr, accd_scr, ms_scr, ls_scr, accs_scr = refs[4 * g:]
    j = pl.program_id(1)
    nj = pl.num_programs(1)
    gp = g * page
    past = nj * gp

    heads = DIFF_HEADS

    row = lax.broadcasted_iota(I32, (QROWS, HEAD_W), 0)
    lane = lax.broadcasted_iota(I32, (QROWS, HEAD_W), 1)
    keep_d = ((row == 0) & (lane < DIFF_QK)) | ((row == 1) & (lane >= DIFF_QK))
    keep_s = row == 0
    kpos = lax.broadcasted_iota(I32, (QROWS, gp), 1) + j * gp
    dist = (past - kpos).astype(F32)

    def head_q(q_ref, h, keep):
        return jnp.where(keep, q_ref[0][:, h * HEAD_W:(h + 1) * HEAD_W].astype(F32), 0.0).astype(BF16)

    def head_rows(refs_, h):
        return jnp.concatenate([r[0, 0, pl.ds(h, page, stride=heads), :] for r in refs_], axis=0).astype(BF16)

    def attend(h, q, k_refs, v_refs, bias, m_scr, l_scr, acc_scr):
        s = _nt_dot(q, head_rows(k_refs, h)) + bias
        m_prev = m_scr[h]
        m_new = jnp.maximum(m_prev, jnp.max(s, axis=1, keepdims=True))
        p = jnp.exp(s - m_new[:, :1])
        alpha = jnp.exp(m_prev - m_new)
        l_scr[h] = alpha * l_scr[h] + jnp.sum(p, axis=1, keepdims=True)
        acc_scr[h] = alpha * acc_scr[h] + jnp.dot(p.astype(BF16), head_rows(v_refs, h),
                                                  preferred_element_type=F32)
        m_scr[h] = m_new

    for h in range(heads):
        hs = slice(h * HEAD_W, (h + 1) * HEAD_W)
        qd = head_q(dq_ref, h, keep_d)
        qs = head_q(sq_ref, h, keep_s)

        @pl.when(j == 0)
        def _():
            sd = jnp.sum(qd.astype(F32) * dkn_ref[0][:, hs].astype(F32), axis=1, keepdims=True)
            md_scr[h] = jnp.broadcast_to(sd, (QROWS, HEAD_W))
            ld_scr[h] = jnp.ones((QROWS, HEAD_W), F32)
            accd_scr[h] = jnp.broadcast_to(dvn_ref[0][:, hs].astype(F32), (QROWS, HEAD_W))
            ss = jnp.sum(qs.astype(F32) * skn_ref[0][:, hs].astype(F32), axis=1, keepdims=True)
            ms_scr[h] = jnp.broadcast_to(ss + new_ref[0][:, :1], (QROWS, HEAD_W))
            ls_scr[h] = jnp.ones((QROWS, HEAD_W), F32)
            accs_scr[h] = jnp.broadcast_to(svn_ref[0][:, hs].astype(F32), (QROWS, HEAD_W))

        attend(h, qd, dk_refs, dv_refs, -SLOPES_DIFF[h] * dist, md_scr, ld_scr, accd_scr)
        attend(h, qs, sk_refs, sv_refs, -SLOPES_DSA[h] * dist + bias_ref[0], ms_scr, ls_scr, accs_scr)

        @pl.when(j == nj - 1)
        def _():
            lam = _diff_lambda(lam_ref[...])
            od = accd_scr[h] / ld_scr[h]
            od_ref[0, :, hs] = _rms(od[0:1] - lam * od[1:2], subln_ref[...]) * (1.0 - LAM_INIT)
            os_ref[0, :, hs] = (accs_scr[h] / ls_scr[h])[0:1]


def _attn_sample(page_table, lam_p, subln, dq, dk_new, dv_new, sq, sk_new, sv_new, selbias, selnew,
                 cache_dk, cache_dv, cache_sk, cache_sv, g):
    nb, n_pages = page_table.shape
    n_pool, page, heads = cache_dk.shape[1:4]
    assert heads == DIFF_HEADS == DSA_HEADS
    gp = g * page
    flat = lambda c: c.reshape(c.shape[0], n_pool, page * heads, HEAD_W)
    vec = lambda a: a.reshape(nb, 1, a.shape[-1])

    def page_spec(i):
        return pl.BlockSpec((1, 1, page * heads, HEAD_W), lambda b, j, pt: (0, pt[b, j * g + i], 0, 0))

    per_b = lambda w: pl.BlockSpec((1, 1, w), lambda b, j, pt: (b, 0, 0))
    const = lambda a: pl.BlockSpec(a.shape, lambda b, j, pt: (0,) * a.ndim)
    pages = [page_spec(i) for i in range(g)]
    state = [pltpu.VMEM((heads, QROWS, HEAD_W), F32)] * 3
    return pl.pallas_call(
        functools.partial(_attn_sample_kernel, g=g, page=page),
        grid_spec=pltpu.PrefetchScalarGridSpec(
            num_scalar_prefetch=1,
            grid=(nb, n_pages // g),
            in_specs=[const(lam_p), const(subln)] + [per_b(WIDTH)] * 6
                     + [pl.BlockSpec((1, 1, gp), lambda b, j, pt: (b, 0, j)), per_b(128)] + pages * 4,
            out_specs=[per_b(WIDTH), per_b(WIDTH)],
            scratch_shapes=state * 2,
        ),
        out_shape=[jax.ShapeDtypeStruct((nb, 1, WIDTH), F32)] * 2,
        compiler_params=pltpu.CompilerParams(dimension_semantics=("arbitrary", "arbitrary"),
                                             vmem_limit_bytes=VMEM_LIMIT),
        name="attn_sample",
    )(page_table, lam_p, subln, vec(dq), vec(dk_new), vec(dv_new), vec(sq), vec(sk_new), vec(sv_new),
      selbias.reshape(nb, 1, -1), vec(selnew),
      *([flat(cache_dk)] * g), *([flat(cache_dv)] * g), *([flat(cache_sk)] * g), *([flat(cache_sv)] * g))


def _silu(x):
    return x / (1.0 + jnp.exp(-x))


def _residual_kernel(h_ref, od_ref, os_ref, dg_ref, sg_ref, p_ref, wo_ref, gp_ref, wg_ref, wp_ref, gf_ref, y_ref):
    a = (od_ref[...] * _silu(dg_ref[...])).astype(BF16)
    b = (os_ref[...] * _silu(sg_ref[...])).astype(BF16)
    h = h_ref[...] + jnp.dot(a, wo_ref[:WIDTH, :], preferred_element_type=F32) \
        + jnp.dot(b, wo_ref[WIDTH:, :], preferred_element_type=F32)
    gate_in = _rms(h, gp_ref[...]).astype(BF16)
    gate = 1.0 / (1.0 + jnp.exp(-jnp.dot(gate_in, wg_ref[...], preferred_element_type=F32)))
    h = h + gate * jnp.dot(p_ref[...].astype(BF16), wp_ref[...], preferred_element_type=F32)
    y_ref[...] = _rms(h, gf_ref[...])


def _residual(h, od, osp, dg, sg, p, w_out, g_ple, w_gate, w_proj, g_final, tm):
    m, d = h.shape
    rows = lambda w: pl.BlockSpec((tm, w), lambda i: (i, 0))
    const = lambda a: pl.BlockSpec(a.shape, lambda i: (0,) * a.ndim)
    return pl.pallas_call(
        _residual_kernel,
        grid=(m // tm,),
        in_specs=[rows(d), rows(WIDTH), rows(WIDTH), rows(WIDTH), rows(WIDTH), rows(p.shape[1]),
                  const(w_out), const(g_ple), const(w_gate), const(w_proj), const(g_final)],
        out_specs=rows(d),
        out_shape=jax.ShapeDtypeStruct((m, d), F32),
        compiler_params=pltpu.CompilerParams(dimension_semantics=("arbitrary",), vmem_limit_bytes=VMEM_LIMIT),
        name="residual",
    )(h, od, osp, dg, sg, p, w_out, g_ple, w_gate, w_proj, g_final)


def _row_tile(m, want):
    t = min(m, want)
    assert m % t == 0
    return t


def kernel(x_prompt, x_sample, cache_diff_k, cache_diff_v, cache_dsa_k, cache_dsa_v, cache_idx_k, page_table,
           p_prompt, p_sample, norm_mix, w_in, diff_lambda, diff_subln, w_out, norm_ple, w_ple_gate, w_ple_proj,
           norm_final):
    batch, seq, d_model = x_prompt.shape
    nb = x_sample.shape[0]
    assert norm_mix.shape[0] == 1 and x_sample.shape[1] == 1
    n_main = N_SEG * SEG
    n_tail = IDX_DIM + IDX_HEADS
    assert w_in.shape[2] == n_main + n_tail

    w = w_in[0]
    w_main = w[:, :n_main].astype(BF16)
    ik_cols = w[:, n_main:n_main + IDX_DIM]
    w_tail = jnp.concatenate(
        [ik_cols, ik_cols, w[:, n_main + IDX_DIM:], jnp.zeros((d_model, TAIL_W - 2 * IDX_DIM - IDX_HEADS), F32)],
        axis=1).astype(BF16)
    g_mix = norm_mix[0][None]
    subln = diff_subln[0][None]
    lam_p = diff_lambda[0]
    res_w = (w_out[0].astype(BF16), norm_ple[0][None], w_ple_gate[0].astype(BF16), w_ple_proj[0].astype(BF16),
             norm_final[None])

    mp = batch * seq
    xp = x_prompt.reshape(mp, d_model)
    (dq, dk32, dk16, dv32, dv16, dg, sq, sk32, sk16, sv32, sv16, sg, iq, ik32, ikk, iw) = _project(
        xp, g_mix, w_main, w_tail, _row_tile(mp, 512))
    t = _row_tile(seq, 256)
    od = _diff_prompt(lam_p, subln, dq, dk16, dv16, batch, seq, t)
    osp = _dsa_prompt(sq, sk16, sv16, iq, ikk, iw, batch, seq, t)
    y_prompt = _residual(xp, od, osp, dg, sg, p_prompt[0].reshape(mp, -1), *res_w, _row_tile(mp, 512))

    xs = x_sample.reshape(nb, d_model)
    (sdq, sdk32, sdk16, sdv32, sdv16, sdg, ssq, ssk32, ssk16, ssv32, ssv16, ssg, siq, sik32, sikk, siw) = _project(
        xs, g_mix, w_main, w_tail, _row_tile(nb, 128))
    g = math.gcd(page_table.shape[1], 8)
    isc = _idx_sample(page_table, siq, siw, cache_idx_k, g).reshape(nb, -1)
    selbias, selnew = _select_sample(isc, siq, sikk, siw, _row_tile(isc.shape[1], 512))
    od_s, os_s = _attn_sample(page_table, lam_p, subln, sdq, sdk16, sdv16, ssq, ssk16, ssv16, selbias, selnew,
                              cache_diff_k, cache_diff_v, cache_dsa_k, cache_dsa_v, g)
    y_sample = _residual(xs, od_s.reshape(nb, WIDTH), os_s.reshape(nb, WIDTH), sdg, ssg,
                         p_sample[0].reshape(nb, -1), *res_w, _row_tile(nb, 128))

    ph = lambda a, hd: a.reshape(1, batch, seq, hd, -1)
    sh = lambda a, hd: a.reshape(1, nb, 1, hd, -1)
    return (y_prompt.reshape(batch, seq, d_model), y_sample.reshape(nb, 1, d_model),
            ph(dk32, DIFF_HEADS), ph(dv32, DIFF_HEADS), ph(sk32, DSA_HEADS), ph(sv32, DSA_HEADS),
            ik32.reshape(1, batch, seq, IDX_DIM),
            sh(sdk32, DIFF_HEADS), sh(sdv32, DIFF_HEADS), sh(ssk32, DSA_HEADS), sh(ssv32, DSA_HEADS),
            sik32.reshape(1, nb, 1, IDX_DIM))
```

```python
import functools
import math

import jax
import jax.numpy as jnp
import numpy as np
from jax import lax
from jax.experimental import pallas as pl
from jax.experimental.pallas import tpu as pltpu

F32 = jnp.float32
BF16 = jnp.bfloat16
I32 = jnp.int32

DIFF_HEADS = 4
DIFF_QK = 64
DSA_HEADS = 4
DSA_DIM = 128
IDX_HEADS = 8
IDX_DIM = 64
HEAD_W = 128
WIDTH = 512
TOPK_MAX = 256
NORM_EPS = 1e-6
SEG = 512
N_SEG = 9
TAIL_W = 256

NEG_BIG = -1e30
INT_MIN = -2 ** 31
INT_MAX = 2 ** 31 - 1
LOG2E = 1.4426950408889634

VMEM_LIMIT = 56 * 1024 * 1024
PROJ_ROWS = 512
Q_TILE = 256
BIG = 1024
HEAD_GROUPS = ((0, 1, 2, 3),)
SUB_ROWS = 64
SELECT_TILE = 512
QROWS = 16
ATTN_PAGES = 8
IDX_PAGES = 16
TOKEN_HEAD_OUTS = (1, 3, 7, 9)


def _alibi_slopes():
    n = DIFF_HEADS + DSA_HEADS
    s = [2.0 ** (-8.0 * i / n) for i in range(1, n + 1)]
    return s[0::2], s[1::2]


SLOPES_DIFF, SLOPES_DSA = _alibi_slopes()
LAM_INIT = 0.8 - 0.6 * math.exp(-0.3 * 0)


def _nt_dot(a, b):
    return lax.dot_general(a, b, (((1,), (1,)), ((), ())), preferred_element_type=F32)


def _rms(x, g):
    ms = jnp.mean(x * x, axis=-1, keepdims=True)
    return x * lax.rsqrt(ms + NORM_EPS) * g


def _float_key(x):
    b = lax.bitcast_convert_type(x, I32)
    return b ^ (lax.shift_right_arithmetic(b, 31) & INT_MAX)


def _store_token_head_rows(ref, z):
    heads = z.shape[1] // HEAD_W
    for h in range(heads):
        ref[pl.ds(h, z.shape[0], stride=heads), :] = z[:, h * HEAD_W:(h + 1) * HEAD_W]


def _proj_kernel(x_ref, g_ref, w_ref, wt_ref,
                 dq_ref, dk32_ref, dk16_ref, dv32_ref, dv16_ref, dg_ref,
                 sq_ref, sk32_ref, sk16_ref, sv32_ref, sv16_ref, sg_ref,
                 iq_ref, ik32_ref, ikk_ref, iw_ref):
    xn = _rms(x_ref[...], g_ref[...]).astype(BF16)

    def seg(i):
        return jnp.dot(xn, w_ref[:, i * SEG:(i + 1) * SEG], preferred_element_type=F32)

    dq_ref[...] = (seg(0) * (DIFF_QK ** -0.5 * LOG2E)).astype(BF16)
    z = seg(1); _store_token_head_rows(dk32_ref, z); dk16_ref[...] = z.astype(BF16)
    z = seg(2); _store_token_head_rows(dv32_ref, z); dv16_ref[...] = z.astype(BF16)
    dg_ref[...] = seg(3)
    sq_ref[...] = (seg(4) * (DSA_DIM ** -0.5 * LOG2E)).astype(BF16)
    z = seg(5); _store_token_head_rows(sk32_ref, z); sk16_ref[...] = z.astype(BF16)
    z = seg(6); _store_token_head_rows(sv32_ref, z); sv16_ref[...] = z.astype(BF16)
    sg_ref[...] = seg(7)
    iq_ref[...] = (seg(8) * IDX_DIM ** -0.5).astype(BF16)
    zt = jnp.dot(xn, wt_ref[...], preferred_element_type=F32)
    ik32_ref[...] = zt[:, :IDX_DIM]
    ikk_ref[...] = zt[:, :2 * IDX_DIM].astype(BF16)
    iw_ref[...] = zt[:, 2 * IDX_DIM:2 * IDX_DIM + IDX_HEADS] * IDX_HEADS ** -0.5


def _project(x, g, w_main, w_tail, tm):
    m, d = x.shape
    row = lambda w, dt: jax.ShapeDtypeStruct((m, w), dt)
    rspec = lambda w: pl.BlockSpec((tm, w), lambda i: (i, 0))
    heads = WIDTH // HEAD_W
    th_spec = pl.BlockSpec((tm * heads, HEAD_W), lambda i: (i, 0))
    th_shape = jax.ShapeDtypeStruct((m * heads, HEAD_W), F32)
    outs = [(WIDTH, BF16),
            (WIDTH, F32), (WIDTH, BF16), (WIDTH, F32), (WIDTH, BF16), (WIDTH, F32),
            (WIDTH, BF16),
            (WIDTH, F32), (WIDTH, BF16), (WIDTH, F32), (WIDTH, BF16), (WIDTH, F32),
            (WIDTH, BF16),
            (IDX_DIM, F32), (2 * IDX_DIM, BF16), (IDX_HEADS, F32)]
    return pl.pallas_call(
        _proj_kernel,
        grid=(m // tm,),
        in_specs=[rspec(d),
                  pl.BlockSpec((1, d), lambda i: (0, 0)),
                  pl.BlockSpec(w_main.shape, lambda i: (0, 0)),
                  pl.BlockSpec(w_tail.shape, lambda i: (0, 0))],
        out_specs=[th_spec if i in TOKEN_HEAD_OUTS else rspec(w) for i, (w, _) in enumerate(outs)],
        out_shape=[th_shape if i in TOKEN_HEAD_OUTS else row(w, dt) for i, (w, dt) in enumerate(outs)],
        compiler_params=pltpu.CompilerParams(dimension_semantics=("arbitrary",),
                                             vmem_limit_bytes=VMEM_LIMIT),
        name="in_proj",
    )(x, g, w_main, w_tail)


def _diff_lambda(lp):
    a = jnp.exp(jnp.sum(lp[0:1] * lp[1:2], axis=-1, keepdims=True))
    b = jnp.exp(jnp.sum(lp[2:3] * lp[3:4], axis=-1, keepdims=True))
    return a - b + LAM_INIT


def _diff_finish(o2, lam, subln, t):
    od = o2[:t] - lam * o2[t:]
    return _rms(od, subln) * (1.0 - LAM_INIT)


def _diff_prompt_kernel(lam_ref, subln_ref, q_ref, k_ref, v_ref, o_ref, m_scr, l_scr, acc_scr, *, t, big):
    qi = pl.program_id(1)
    lam = _diff_lambda(lam_ref[...])
    r = big // t
    n_big = qi // r
    row = lax.broadcasted_iota(I32, (2 * t, HEAD_W), 0)
    row = jnp.where(row >= t, row - t, row)
    rowf = row.astype(F32)
    colf = lax.broadcasted_iota(I32, (1, big), 1).astype(F32)
    lane = lax.broadcasted_iota(I32, (t, HEAD_W), 1)
    rel = lax.broadcasted_iota(I32, (2 * t, big), 1) - jnp.tile(row, (1, big // HEAD_W))
    visible = rel <= qi * t - n_big * big
    n = big // HEAD_W

    m_scr[...] = jnp.full(m_scr.shape, -jnp.inf, F32)
    l_scr[...] = jnp.zeros(l_scr.shape, F32)
    acc_scr[...] = jnp.zeros(acc_scr.shape, F32)
    qqs = []
    for h in range(DIFF_HEADS):
        qh = q_ref[:, h * HEAD_W:(h + 1) * HEAD_W].astype(F32)
        qqs.append(jnp.concatenate([jnp.where(lane < DIFF_QK, qh, 0.0), jnp.where(lane >= DIFF_QK, qh, 0.0)],
                                   axis=0).astype(BF16))

    def tile(jb, mask):
        kb = k_ref[pl.ds(jb * r, r)].reshape(big, WIDTH)
        vb = v_ref[pl.ds(jb * r, r)].reshape(big, WIDTH)
        dist = (qi * t - jb * big).astype(F32)
        for grp in HEAD_GROUPS:
            us = {h: _nt_dot(qqs[h], kb[:, h * HEAD_W:(h + 1) * HEAD_W]) for h in grp}
            ps = {}
            for h in grp:
                slope = SLOPES_DIFF[h] * LOG2E
                u = us[h] + slope * colf
                if mask is not None:
                    u = jnp.where(mask, u, NEG_BIG)
                shift = slope * rowf + slope * dist
                part = u[:, :HEAD_W]
                for c in range(1, n):
                    part = jnp.maximum(part, u[:, c * HEAD_W:(c + 1) * HEAD_W])
                m_prev = m_scr[h]
                m_new = jnp.maximum(m_prev, jnp.max(part, axis=1, keepdims=True) - shift)
                p = jnp.exp2(u - jnp.tile(m_new + shift, (1, n)))
                alpha = jnp.exp2(m_prev - m_new)
                psum = p[:, :HEAD_W]
                for c in range(1, n):
                    psum = psum + p[:, c * HEAD_W:(c + 1) * HEAD_W]
                l_scr[h] = alpha * l_scr[h] + psum
                m_scr[h] = m_new
                ps[h] = (p.astype(BF16), alpha)
            for h in grp:
                p, alpha = ps[h]
                acc_scr[h] = alpha * acc_scr[h] + jnp.dot(p, vb[:, h * HEAD_W:(h + 1) * HEAD_W],
                                                          preferred_element_type=F32)

    def body(jb, c):
        tile(jb, None)
        return c

    lax.fori_loop(0, n_big, body, 0)
    tile(n_big, visible)
    for h in range(DIFF_HEADS):
        o2 = acc_scr[h] / jnp.sum(l_scr[h], axis=1, keepdims=True)
        o_ref[:, h * HEAD_W:(h + 1) * HEAD_W] = _diff_finish(o2, lam, subln_ref[...], t)


def _diff_prompt(lam_p, subln, dq, dk, dv, batch, seq, t):
    nt = seq // t
    k3 = dk.reshape(batch * nt, t, WIDTH)
    v3 = dv.reshape(batch * nt, t, WIDTH)
    kv_spec = pl.BlockSpec((nt, t, WIDTH), lambda b, i: (b, 0, 0), pipeline_mode=pl.Buffered(1))
    return pl.pallas_call(
        functools.partial(_diff_prompt_kernel, t=t, big=BIG),
        grid=(batch, nt),
        in_specs=[pl.BlockSpec(lam_p.shape, lambda b, i: (0, 0)),
                  pl.BlockSpec(subln.shape, lambda b, i: (0, 0)),
                  pl.BlockSpec((t, WIDTH), lambda b, i: (b * nt + i, 0)),
                  kv_spec, kv_spec],
        out_specs=pl.BlockSpec((t, WIDTH), lambda b, i: (b * nt + i, 0)),
        out_shape=jax.ShapeDtypeStruct((batch * seq, WIDTH), F32),
        scratch_shapes=[pltpu.VMEM((DIFF_HEADS, 2 * t, HEAD_W), F32)] * 3,
        compiler_params=pltpu.CompilerParams(dimension_semantics=("arbitrary", "arbitrary"),
                                             vmem_limit_bytes=VMEM_LIMIT),
        name="diff_prompt",
    )(lam_p, subln, dq, k3, v3)


def _count_keys(key_ref, first, n_big, r, rows, pred):
    tk = key_ref.shape[2]
    lane = lax.broadcasted_iota(I32, (rows, 128), 1)

    def body(jb, cnt):
        for i in range(r):
            for c in range(tk // 128):
                kt = key_ref[jb * r + i, pl.ds(first, rows), c * 128:(c + 1) * 128]
                cnt = cnt + pred(kt, lane + ((jb * r + i) * tk + c * 128)).astype(I32)
        return cnt

    cnt = lax.fori_loop(0, n_big, body, jnp.zeros((rows, 128), I32))
    return jnp.sum(cnt, axis=1, keepdims=True)


def _select_rows(key_ref, thr_ref, extra_ref, first, n_big, r, rows, ksel, idx_bits):
    tk = key_ref.shape[2]
    extra_pos = key_ref.shape[0] * tk

    def count(pred):
        tot = _count_keys(key_ref, first, n_big, r, rows, pred)
        if extra_ref is not None:
            tot = tot + pred(extra_ref[pl.ds(first, rows), :1], extra_pos).astype(I32)
        return tot

    def unsettled(n_at):
        return jnp.max(jnp.where(n_at != ksel, 1, 0))

    def value_pass(state):
        i, thr, n_at, _ = state
        cand = thr + lax.shift_left(jnp.int32(1), 31 - i)
        n_ge = count(lambda kt, pos: kt >= cand[:, :kt.shape[1]])
        ok = n_ge >= ksel
        n_at = jnp.where(ok, n_ge, n_at)
        return i + 1, jnp.where(ok, cand, thr), n_at, unsettled(n_at)

    start = (jnp.int32(0), jnp.full((rows, 128), INT_MIN, I32), jnp.full((rows, 1), INT_MAX, I32), jnp.int32(1))
    _, thr, n_at, _ = lax.while_loop(lambda st: (st[0] < 32) & (st[3] > 0), value_pass, start)
    thr_ref[pl.ds(first, rows), :] = thr

    @pl.when(jnp.max(n_at) > ksel)
    def _():
        need = ksel - count(lambda kt, pos: kt > thr[:, :kt.shape[1]])

        def pos_pass(i, p):
            cand = p + lax.shift_left(jnp.int32(1), idx_bits - 1 - i)
            n_lt = count(lambda kt, pos: (kt == thr[:, :kt.shape[1]]) & (pos < cand[:, :kt.shape[1]]))
            return jnp.where(n_lt < need, cand, p)

        last = lax.fori_loop(0, idx_bits, pos_pass, jnp.zeros((rows, 128), I32))
        lane = lax.broadcasted_iota(I32, (rows, 128), 1)

        def demote(j, c):
            for ch in range(tk // 128):
                kt = key_ref[j, pl.ds(first, rows), ch * 128:(ch + 1) * 128]
                drop = (kt == thr) & (lane + (j * tk + ch * 128) > last)
                key_ref[j, pl.ds(first, rows), ch * 128:(ch + 1) * 128] = jnp.where(drop, INT_MIN, kt)
            return c

        lax.fori_loop(0, n_big * r, demote, 0)
        if extra_ref is not None:
            e = extra_ref[pl.ds(first, rows), :]
            extra_ref[pl.ds(first, rows), :] = jnp.where((e == thr) & (extra_pos > last), INT_MIN, e)


def _dsa_prompt_kernel(q_ref, k_ref, v_ref, iq_ref, ikk_ref, iw_ref, o_ref,
                       key_scr, thr_scr, m_scr, l_scr, acc_scr, *, t, big, sub, ksel, idx_bits):
    qi = pl.program_id(1)
    r = big // t
    n_big = qi // r + 1
    row = lax.broadcasted_iota(I32, (t, t), 0)
    col = lax.broadcasted_iota(I32, (t, t), 1)
    causal = col <= row
    lane = lax.broadcasted_iota(I32, (t, HEAD_W), 1)

    iqs, ws = [], []
    for h in range(IDX_HEADS):
        pair = iq_ref[:, (h // 2) * HEAD_W:(h // 2 + 1) * HEAD_W].astype(F32)
        keep = (lane < IDX_DIM) if h % 2 == 0 else (lane >= IDX_DIM)
        iqs.append(jnp.where(keep, pair, 0.0).astype(BF16))
        ws.append(jnp.broadcast_to(iw_ref[:, h:h + 1], (t, t)))

    def index_tile(j, masked):
        ikt = ikk_ref[j]
        acc = jnp.zeros((t, t), F32)
        for h in range(IDX_HEADS):
            acc = acc + ws[h] * jnp.maximum(_nt_dot(iqs[h], ikt), 0.0)
        if masked:
            acc = jnp.where(causal, acc, -jnp.inf)
        key_scr[j] = _float_key(acc)

    def index_body(j, c):
        index_tile(j, False)
        return c

    lax.fori_loop(0, qi, index_body, 0)
    index_tile(qi, True)

    def pad_body(j, c):
        key_scr[j] = jnp.full((t, t), INT_MIN, I32)
        return c

    lax.fori_loop(qi + 1, n_big * r, pad_body, 0)

    def select_body(i, c):
        _select_rows(key_scr, thr_scr, None, pl.multiple_of(i * sub, sub), n_big, r, sub, ksel, idx_bits)
        return c

    lax.fori_loop(0, t // sub, select_body, 0)

    n = big // HEAD_W
    thr = jnp.tile(thr_scr[...], (1, n))
    rowf = lax.broadcasted_iota(I32, (t, HEAD_W), 0).astype(F32)
    colf = lax.broadcasted_iota(I32, (1, big), 1).astype(F32)
    rel = lax.broadcasted_iota(I32, (t, big), 1) - lax.broadcasted_iota(I32, (t, big), 0)
    visible = rel <= qi * t - (n_big - 1) * big
    m_scr[...] = jnp.full(m_scr.shape, -jnp.inf, F32)
    l_scr[...] = jnp.zeros(l_scr.shape, F32)
    acc_scr[...] = jnp.zeros(acc_scr.shape, F32)

    def attend_tile(jb, mask):
        keys = jnp.concatenate([key_scr[jb * r + i] for i in range(r)], axis=1)
        sel = keys >= thr
        if mask is not None:
            sel = sel & mask
        selbias = jnp.where(sel, 0.0, NEG_BIG)
        kb = k_ref[pl.ds(jb * r, r)].reshape(big, WIDTH)
        vb = v_ref[pl.ds(jb * r, r)].reshape(big, WIDTH)
        dist = (qi * t - jb * big).astype(F32)
        for grp in HEAD_GROUPS:
            us = {h: _nt_dot(q_ref[:, h * HEAD_W:(h + 1) * HEAD_W], kb[:, h * HEAD_W:(h + 1) * HEAD_W]) for h in grp}
            ps = {}
            for h in grp:
                slope = SLOPES_DSA[h] * LOG2E
                u = us[h] + slope * colf + selbias
                shift = slope * rowf + slope * dist
                part = u[:, :HEAD_W]
                for c in range(1, n):
                    part = jnp.maximum(part, u[:, c * HEAD_W:(c + 1) * HEAD_W])
                m_prev = m_scr[h]
                m_new = jnp.maximum(m_prev, jnp.max(part, axis=1, keepdims=True) - shift)
                p = jnp.exp2(u - jnp.tile(m_new + shift, (1, n)))
                alpha = jnp.exp2(m_prev - m_new)
                psum = p[:, :HEAD_W]
                for c in range(1, n):
                    psum = psum + p[:, c * HEAD_W:(c + 1) * HEAD_W]
                l_scr[h] = alpha * l_scr[h] + psum
                m_scr[h] = m_new
                ps[h] = (p.astype(BF16), alpha)
            for h in grp:
                p, alpha = ps[h]
                acc_scr[h] = alpha * acc_scr[h] + jnp.dot(p, vb[:, h * HEAD_W:(h + 1) * HEAD_W],
                                                          preferred_element_type=F32)

    def attend_body(jb, c):
        attend_tile(jb, None)
        return c

    lax.fori_loop(0, n_big - 1, attend_body, 0)
    attend_tile(n_big - 1, visible)
    for h in range(DSA_HEADS):
        o_ref[:, h * HEAD_W:(h + 1) * HEAD_W] = acc_scr[h] / jnp.sum(l_scr[h], axis=1, keepdims=True)


def _dsa_prompt(sq, sk, sv, iq, ikk, iw, batch, seq, t):
    nt = seq // t
    ksel = min(TOPK_MAX, seq // 4)
    assert t >= ksel and t % 128 == 0
    idx_bits = max(1, int(seq).bit_length())
    whole = lambda w: pl.BlockSpec((nt, t, w), lambda b, i: (b, 0, 0), pipeline_mode=pl.Buffered(1))
    rows = lambda w: pl.BlockSpec((t, w), lambda b, i: (b * nt + i, 0))
    return pl.pallas_call(
        functools.partial(_dsa_prompt_kernel, t=t, big=BIG, sub=SUB_ROWS, ksel=ksel, idx_bits=idx_bits),
        grid=(batch, nt),
        in_specs=[rows(WIDTH), whole(WIDTH), whole(WIDTH), rows(WIDTH), whole(2 * IDX_DIM), rows(IDX_HEADS)],
        out_specs=rows(WIDTH),
        out_shape=jax.ShapeDtypeStruct((batch * seq, WIDTH), F32),
        scratch_shapes=[pltpu.VMEM((nt, t, t), I32), pltpu.VMEM((t, HEAD_W), I32),
                        pltpu.VMEM((DSA_HEADS, t, HEAD_W), F32), pltpu.VMEM((DSA_HEADS, t, HEAD_W), F32),
                        pltpu.VMEM((DSA_HEADS, t, HEAD_W), F32)],
        compiler_params=pltpu.CompilerParams(dimension_semantics=("arbitrary", "arbitrary"),
                                             vmem_limit_bytes=VMEM_LIMIT),
        name="dsa_prompt",
    )(sq, sk.reshape(batch * nt, t, WIDTH), sv.reshape(batch * nt, t, WIDTH), iq,
      ikk.reshape(batch * nt, t, 2 * IDX_DIM), iw)


def _idx_sample_kernel(pt_ref, iq_ref, iw_ref, *refs, g):
    page_refs, o_ref = refs[:g], refs[g]
    pad = QROWS - IDX_HEADS
    iq = jnp.concatenate([iq_ref[0].astype(F32), jnp.zeros((pad, IDX_DIM), F32)], axis=0).astype(BF16)
    w = jnp.concatenate([iw_ref[0], jnp.zeros((pad, 1), F32)], axis=0)
    for i in range(g):
        keys = page_refs[i][0, 0].astype(BF16)
        d = jnp.dot(iq, keys, preferred_element_type=F32)
        page = keys.shape[1]
        o_ref[0, :, i * page:(i + 1) * page] = jnp.sum(w * jnp.maximum(d, 0.0), axis=0, keepdims=True)


def _idx_sample(page_table, iq, iw, cache_ik, g):
    nb, n_pages = page_table.shape
    page = cache_ik.shape[2]
    cache_t = jnp.swapaxes(cache_ik, 2, 3)

    def page_spec(i):
        return pl.BlockSpec((1, 1, IDX_DIM, page), lambda b, j, pt: (0, pt[b, j * g + i], 0, 0))

    return pl.pallas_call(
        functools.partial(_idx_sample_kernel, g=g),
        grid_spec=pltpu.PrefetchScalarGridSpec(
            num_scalar_prefetch=1,
            grid=(nb, n_pages // g),
            in_specs=[pl.BlockSpec((1, IDX_HEADS, IDX_DIM), lambda b, j, pt: (b, 0, 0)),
                      pl.BlockSpec((1, IDX_HEADS, 1), lambda b, j, pt: (b, 0, 0))]
                     + [page_spec(i) for i in range(g)],
            out_specs=pl.BlockSpec((1, 1, g * page), lambda b, j, pt: (b, 0, j)),
        ),
        out_shape=jax.ShapeDtypeStruct((nb, 1, n_pages * page), F32),
        compiler_params=pltpu.CompilerParams(dimension_semantics=("arbitrary", "arbitrary"),
                                             vmem_limit_bytes=VMEM_LIMIT),
        name="idx_sample",
    )(page_table, iq.reshape(nb, IDX_HEADS, IDX_DIM), iw.reshape(nb, IDX_HEADS, 1), *([cache_t] * g))


def _select_sample_kernel(isc_ref, iq_ref, ikk_ref, iw_ref, bias_ref, new_ref, key_scr, thr_scr, extra_scr,
                          *, tk, sub, ksel, idx_bits):
    rows, past = isc_ref.shape
    nt = past // tk
    for j in range(nt):
        key_scr[j] = _float_key(isc_ref[:, j * tk:(j + 1) * tk] + 0.0)
    lane = lax.broadcasted_iota(I32, (rows, HEAD_W), 1)
    kk = ikk_ref[...].astype(F32)
    new = jnp.zeros((rows, 1), F32)
    for h in range(IDX_HEADS):
        pair = iq_ref[:, (h // 2) * HEAD_W:(h // 2 + 1) * HEAD_W].astype(F32)
        keep = (lane < IDX_DIM) if h % 2 == 0 else (lane >= IDX_DIM)
        d = jnp.sum(jnp.where(keep, pair * kk, 0.0), axis=1, keepdims=True)
        new = new + iw_ref[:, h:h + 1] * jnp.maximum(d, 0.0)
    extra_scr[...] = jnp.broadcast_to(_float_key(new), extra_scr.shape)

    def select_body(i, c):
        _select_rows(key_scr, thr_scr, extra_scr, pl.multiple_of(i * sub, sub), nt, 1, sub, ksel, idx_bits)
        return c

    lax.fori_loop(0, rows // sub, select_body, 0)
    thr = jnp.tile(thr_scr[...], (1, tk // 128))
    for j in range(nt):
        bias_ref[:, j * tk:(j + 1) * tk] = jnp.where(key_scr[j] >= thr, 0.0, NEG_BIG)
    new_ref[...] = jnp.where(extra_scr[...] >= thr_scr[...], 0.0, NEG_BIG)


def _select_sample(isc, iq, ikk, iw, tk):
    rows, past = isc.shape
    ksel = min(TOPK_MAX, (past + 1) // 4)
    idx_bits = int(past + 1).bit_length()
    full = lambda a: pl.BlockSpec(a.shape, lambda i: (0,) * a.ndim)
    return pl.pallas_call(
        functools.partial(_select_sample_kernel, tk=tk, sub=min(SUB_ROWS, rows), ksel=ksel, idx_bits=idx_bits),
        grid=(1,),
        in_specs=[full(isc), full(iq), full(ikk), full(iw)],
        out_specs=[pl.BlockSpec((rows, past), lambda i: (0, 0)), pl.BlockSpec((rows, 128), lambda i: (0, 0))],
        out_shape=[jax.ShapeDtypeStruct((rows, past), F32), jax.ShapeDtypeStruct((rows, 128), F32)],
        scratch_shapes=[pltpu.VMEM((past // tk, rows, tk), I32), pltpu.VMEM((rows, 128), I32),
                        pltpu.VMEM((rows, 128), I32)],
        compiler_params=pltpu.CompilerParams(dimension_semantics=("arbitrary",), vmem_limit_bytes=VMEM_LIMIT),
        name="select_sample",
    )(isc, iq, ikk, iw)


def _attn_sample_kernel(pt_ref, lam_ref, subln_ref, dq_ref, dkn_ref, dvn_ref, sq_ref, skn_ref, svn_ref,
                        bias_ref, new_ref, *refs, g, page):
    dk_refs, dv_refs, sk_refs, sv_refs = refs[:g], refs[g:2 * g], refs[2 * g:3 * g], refs[3 * g:4 * g]
    od_ref, os_ref, md_scr, ld_scr, accd_scr, ms_scr, ls_scr, accs_scr = refs[4 * g:]
    j = pl.program_id(1)
    nj = pl.num_programs(1)
    gp = g * page
    past = nj * gp
    heads = DIFF_HEADS
    row = lax.broadcasted_iota(I32, (QROWS, HEAD_W), 0)
    lane = lax.broadcasted_iota(I32, (QROWS, HEAD_W), 1)
    keep_d = ((row == 0) & (lane < DIFF_QK)) | ((row == 1) & (lane >= DIFF_QK))
    keep_s = row == 0
    kpos = lax.broadcasted_iota(I32, (QROWS, gp), 1) + j * gp
    dist = (past - kpos).astype(F32)

    def head_q(q_ref, h, keep):
        return jnp.where(keep, q_ref[0][:, h * HEAD_W:(h + 1) * HEAD_W].astype(F32), 0.0).astype(BF16)

    def head_rows(refs_, h):
        return jnp.concatenate([r[0, 0, pl.ds(h, page, stride=heads), :] for r in refs_], axis=0).astype(BF16)

    qds = [head_q(dq_ref, h, keep_d) for h in range(heads)]
    qss = [head_q(sq_ref, h, keep_s) for h in range(heads)]

    @pl.when(j == 0)
    def _():
        for h in range(heads):
            hs = slice(h * HEAD_W, (h + 1) * HEAD_W)
            sd = jnp.sum(qds[h].astype(F32) * dkn_ref[0][:, hs].astype(F32), axis=1, keepdims=True)
            md_scr[h] = jnp.broadcast_to(sd, (QROWS, HEAD_W))
            ld_scr[h] = jnp.ones((QROWS, HEAD_W), F32)
            accd_scr[h] = jnp.broadcast_to(dvn_ref[0][:, hs].astype(F32), (QROWS, HEAD_W))
            ss = jnp.sum(qss[h].astype(F32) * skn_ref[0][:, hs].astype(F32), axis=1, keepdims=True)
            ms_scr[h] = jnp.broadcast_to(ss + new_ref[0][:, :1], (QROWS, HEAD_W))
            ls_scr[h] = jnp.ones((QROWS, HEAD_W), F32)
            accs_scr[h] = jnp.broadcast_to(svn_ref[0][:, hs].astype(F32), (QROWS, HEAD_W))

    selbias = bias_ref[0]
    jobs = []
    for h in range(heads):
        jobs.append((h, qds[h], dk_refs, dv_refs, -(SLOPES_DIFF[h] * LOG2E) * dist, md_scr, ld_scr, accd_scr))
        jobs.append((h, qss[h], sk_refs, sv_refs, -(SLOPES_DSA[h] * LOG2E) * dist + selbias, ms_scr, ls_scr, accs_scr))
    scores = [_nt_dot(q, head_rows(k_refs, h)) + bias for h, q, k_refs, _, bias, _, _, _ in jobs]
    probs = []
    for (h, _, _, _, _, m_scr, l_scr, acc_scr), s in zip(jobs, scores):
        m_prev = m_scr[h]
        m_new = jnp.maximum(m_prev, jnp.max(s, axis=1, keepdims=True))
        p = jnp.exp2(s - m_new[:, :1])
        alpha = jnp.exp2(m_prev - m_new)
        l_scr[h] = alpha * l_scr[h] + jnp.sum(p, axis=1, keepdims=True)
        m_scr[h] = m_new
        probs.append((p.astype(BF16), alpha))
    for (h, _, _, v_refs, _, _, _, acc_scr), (p, alpha) in zip(jobs, probs):
        acc_scr[h] = alpha * acc_scr[h] + jnp.dot(p, head_rows(v_refs, h), preferred_element_type=F32)

    @pl.when(j == nj - 1)
    def _():
        lam = _diff_lambda(lam_ref[...])
        for h in range(heads):
            hs = slice(h * HEAD_W, (h + 1) * HEAD_W)
            od = accd_scr[h] / ld_scr[h]
            od_ref[0, :, hs] = _rms(od[0:1] - lam * od[1:2], subln_ref[...]) * (1.0 - LAM_INIT)
            os_ref[0, :, hs] = (accs_scr[h] / ls_scr[h])[0:1]


def _attn_sample(page_table, lam_p, subln, dq, dk_new, dv_new, sq, sk_new, sv_new, selbias, selnew,
                 cache_dk, cache_dv, cache_sk, cache_sv, g):
    nb, n_pages = page_table.shape
    n_pool, page, heads = cache_dk.shape[1:4]
    assert heads == DIFF_HEADS == DSA_HEADS
    gp = g * page
    flat = lambda c: c.reshape(c.shape[0], n_pool, page * heads, HEAD_W)
    vec = lambda a: a.reshape(nb, 1, a.shape[-1])

    def page_spec(i):
        return pl.BlockSpec((1, 1, page * heads, HEAD_W), lambda b, j, pt: (0, pt[b, j * g + i], 0, 0))

    per_b = lambda w: pl.BlockSpec((1, 1, w), lambda b, j, pt: (b, 0, 0))
    const = lambda a: pl.BlockSpec(a.shape, lambda b, j, pt: (0,) * a.ndim)
    pages = [page_spec(i) for i in range(g)]
    state = [pltpu.VMEM((heads, QROWS, HEAD_W), F32)] * 3
    return pl.pallas_call(
        functools.partial(_attn_sample_kernel, g=g, page=page),
        grid_spec=pltpu.PrefetchScalarGridSpec(
            num_scalar_prefetch=1,
            grid=(nb, n_pages // g),
            in_specs=[const(lam_p), const(subln)] + [per_b(WIDTH)] * 6
                     + [pl.BlockSpec((1, 1, gp), lambda b, j, pt: (b, 0, j)), per_b(128)] + pages * 4,
            out_specs=[per_b(WIDTH), per_b(WIDTH)],
            scratch_shapes=state * 2,
        ),
        out_shape=[jax.ShapeDtypeStruct((nb, 1, WIDTH), F32)] * 2,
        compiler_params=pltpu.CompilerParams(dimension_semantics=("arbitrary", "arbitrary"),
                                             vmem_limit_bytes=VMEM_LIMIT),
        name="attn_sample",
    )(page_table, lam_p, subln, vec(dq), vec(dk_new), vec(dv_new), vec(sq), vec(sk_new), vec(sv_new),
      selbias.reshape(nb, 1, -1), vec(selnew),
      *([flat(cache_dk)] * g), *([flat(cache_dv)] * g), *([flat(cache_sk)] * g), *([flat(cache_sv)] * g))


def _silu(x):
    return x / (1.0 + jnp.exp(-x))


def _residual_kernel(h_ref, od_ref, os_ref, dg_ref, sg_ref, p_ref, wo_ref, gp_ref, wg_ref, wp_ref, gf_ref, y_ref):
    a = (od_ref[...] * _silu(dg_ref[...])).astype(BF16)
    b = (os_ref[...] * _silu(sg_ref[...])).astype(BF16)
    h = h_ref[...] + jnp.dot(a, wo_ref[:WIDTH, :], preferred_element_type=F32) \
        + jnp.dot(b, wo_ref[WIDTH:, :], preferred_element_type=F32)
    gate_in = _rms(h, gp_ref[...]).astype(BF16)
    gate = 1.0 / (1.0 + jnp.exp(-jnp.dot(gate_in, wg_ref[...], preferred_element_type=F32)))
    h = h + gate * jnp.dot(p_ref[...].astype(BF16), wp_ref[...], preferred_element_type=F32)
    y_ref[...] = _rms(h, gf_ref[...])


def _residual(h, od, osp, dg, sg, p, w_out, g_ple, w_gate, w_proj, g_final, tm):
    m, d = h.shape
    rows = lambda w: pl.BlockSpec((tm, w), lambda i: (i, 0))
    const = lambda a: pl.BlockSpec(a.shape, lambda i: (0,) * a.ndim)
    return pl.pallas_call(
        _residual_kernel,
        grid=(m // tm,),
        in_specs=[rows(d), rows(WIDTH), rows(WIDTH), rows(WIDTH), rows(WIDTH), rows(p.shape[1]),
                  const(w_out), const(g_ple), const(w_gate), const(w_proj), const(g_final)],
        out_specs=rows(d),
        out_shape=jax.ShapeDtypeStruct((m, d), F32),
        compiler_params=pltpu.CompilerParams(dimension_semantics=("arbitrary",), vmem_limit_bytes=VMEM_LIMIT),
        name="residual",
    )(h, od, osp, dg, sg, p, w_out, g_ple, w_gate, w_proj, g_final)


def _row_tile(m, want):
    t = min(m, want)
    assert m % t == 0
    return t


def kernel(x_prompt, x_sample, cache_diff_k, cache_diff_v, cache_dsa_k, cache_dsa_v, cache_idx_k, page_table,
           p_prompt, p_sample, norm_mix, w_in, diff_lambda, diff_subln, w_out, norm_ple, w_ple_gate, w_ple_proj,
           norm_final):
    batch, seq, d_model = x_prompt.shape
    nb = x_sample.shape[0]
    assert norm_mix.shape[0] == 1 and x_sample.shape[1] == 1
    n_main = N_SEG * SEG
    n_tail = IDX_DIM + IDX_HEADS
    assert w_in.shape[2] == n_main + n_tail

    w = w_in[0]
    w_main = w[:, :n_main].astype(BF16)
    ik_cols = w[:, n_main:n_main + IDX_DIM]
    w_tail = jnp.concatenate(
        [ik_cols, ik_cols, w[:, n_main + IDX_DIM:], jnp.zeros((d_model, TAIL_W - 2 * IDX_DIM - IDX_HEADS), F32)],
        axis=1).astype(BF16)
    g_mix = norm_mix[0][None]
    subln = diff_subln[0][None]
    lam_p = diff_lambda[0]
    res_w = (w_out[0].astype(BF16), norm_ple[0][None], w_ple_gate[0].astype(BF16), w_ple_proj[0].astype(BF16),
             norm_final[None])

    mp = batch * seq
    xp = x_prompt.reshape(mp, d_model)
    (dq, dk32, dk16, dv32, dv16, dg, sq, sk32, sk16, sv32, sv16, sg, iq, ik32, ikk, iw) = _project(
        xp, g_mix, w_main, w_tail, _row_tile(mp, PROJ_ROWS))
    t = _row_tile(seq, Q_TILE)
    assert seq % BIG == 0 and BIG % t == 0
    od = _diff_prompt(lam_p, subln, dq, dk16, dv16, batch, seq, t)
    osp = _dsa_prompt(sq, sk16, sv16, iq, ikk, iw, batch, seq, t)
    y_prompt = _residual(xp, od, osp, dg, sg, p_prompt[0].reshape(mp, -1), *res_w, _row_tile(mp, PROJ_ROWS))

    xs = x_sample.reshape(nb, d_model)
    (sdq, sdk32, sdk16, sdv32, sdv16, sdg, ssq, ssk32, ssk16, ssv32, ssv16, ssg, siq, sik32, sikk, siw) = _project(
        xs, g_mix, w_main, w_tail, _row_tile(nb, PROJ_ROWS))
    n_pages = page_table.shape[1]
    isc = _idx_sample(page_table, siq, siw, cache_idx_k, math.gcd(n_pages, IDX_PAGES)).reshape(nb, -1)
    selbias, selnew = _select_sample(isc, siq, sikk, siw, _row_tile(isc.shape[1], SELECT_TILE))
    od_s, os_s = _attn_sample(page_table, lam_p, subln, sdq, sdk16, sdv16, ssq, ssk16, ssv16, selbias, selnew,
                              cache_diff_k, cache_diff_v, cache_dsa_k, cache_dsa_v, math.gcd(n_pages, ATTN_PAGES))
    y_sample = _residual(xs, od_s.reshape(nb, WIDTH), os_s.reshape(nb, WIDTH), sdg, ssg,
                         p_sample[0].reshape(nb, -1), *res_w, _row_tile(nb, PROJ_ROWS))

    ph = lambda a, hd: a.reshape(1, batch, seq, hd, -1)
    sh = lambda a, hd: a.reshape(1, nb, 1, hd, -1)
    return (y_prompt.reshape(batch, seq, d_model), y_sample.reshape(nb, 1, d_model),
            ph(dk32, DIFF_HEADS), ph(dv32, DIFF_HEADS), ph(sk32, DSA_HEADS), ph(sv32, DSA_HEADS),
            ik32.reshape(1, batch, seq, IDX_DIM),
            sh(sdk32, DIFF_HEADS), sh(sdv32, DIFF_HEADS), sh(ssk32, DSA_HEADS), sh(ssv32, DSA_HEADS),
            sik32.reshape(1, nb, 1, IDX_DIM))
```

```python
import functools
import math

import jax
import jax.numpy as jnp
import numpy as np
from jax import lax
from jax.experimental import pallas as pl
from jax.experimental.pallas import tpu as pltpu

F32 = jnp.float32
BF16 = jnp.bfloat16
I32 = jnp.int32

DIFF_HEADS = 4
DIFF_QK = 64
DSA_HEADS = 4
DSA_DIM = 128
IDX_HEADS = 8
IDX_DIM = 64
HEAD_W = 128
WIDTH = 512
TOPK_MAX = 256
NORM_EPS = 1e-6
SEG = 512
N_SEG = 9
TAIL_W = 256

NEG_BIG = -1e30
INT_MIN = -2 ** 31
INT_MAX = 2 ** 31 - 1
ZERO_TOP = 0x007FFFFF
LOG2E = 1.4426950408889634

VMEM_LIMIT = 56 * 1024 * 1024
PROJ_ROWS = 512
Q_TILE = 256
BIG = 1024
HEAD_GROUPS = ((0, 1, 2, 3),)
SUB_ROWS = 64
SELECT_TILE = 512
QROWS = 16
ATTN_PAGES = 8
IDX_PAGES = 16
TOKEN_HEAD_OUTS = (1, 3, 7, 9)


def _alibi_slopes():
    n = DIFF_HEADS + DSA_HEADS
    s = [2.0 ** (-8.0 * i / n) for i in range(1, n + 1)]
    return s[0::2], s[1::2]


SLOPES_DIFF, SLOPES_DSA = _alibi_slopes()
LAM_INIT = 0.8 - 0.6 * math.exp(-0.3 * 0)


def _nt_dot(a, b):
    return lax.dot_general(a, b, (((1,), (1,)), ((), ())), preferred_element_type=F32)


def _rms(x, g):
    ms = jnp.mean(x * x, axis=-1, keepdims=True)
    return x * lax.rsqrt(ms + NORM_EPS) * g


def _float_key(x):
    b = lax.bitcast_convert_type(x, I32)
    return b ^ (lax.shift_right_arithmetic(b, 31) & INT_MAX)


def _score_key(x, pos):
    return jnp.where(x == 0.0, ZERO_TOP - pos, _float_key(x))


def _store_token_head_rows(ref, z):
    heads = z.shape[1] // HEAD_W
    for h in range(heads):
        ref[pl.ds(h, z.shape[0], stride=heads), :] = z[:, h * HEAD_W:(h + 1) * HEAD_W]


def _proj_kernel(x_ref, g_ref, w_ref, wt_ref,
                 dq_ref, dk32_ref, dk16_ref, dv32_ref, dv16_ref, dg_ref,
                 sq_ref, sk32_ref, sk16_ref, sv32_ref, sv16_ref, sg_ref,
                 iq_ref, ik32_ref, ikk_ref, iw_ref):
    xn = _rms(x_ref[...], g_ref[...]).astype(BF16)

    def seg(i):
        return jnp.dot(xn, w_ref[:, i * SEG:(i + 1) * SEG], preferred_element_type=F32)

    dq_ref[...] = (seg(0) * (DIFF_QK ** -0.5 * LOG2E)).astype(BF16)
    z = seg(1); _store_token_head_rows(dk32_ref, z); dk16_ref[...] = z.astype(BF16)
    z = seg(2); _store_token_head_rows(dv32_ref, z); dv16_ref[...] = z.astype(BF16)
    dg_ref[...] = seg(3)
    sq_ref[...] = (seg(4) * (DSA_DIM ** -0.5 * LOG2E)).astype(BF16)
    z = seg(5); _store_token_head_rows(sk32_ref, z); sk16_ref[...] = z.astype(BF16)
    z = seg(6); _store_token_head_rows(sv32_ref, z); sv16_ref[...] = z.astype(BF16)
    sg_ref[...] = seg(7)
    iq_ref[...] = (seg(8) * IDX_DIM ** -0.5).astype(BF16)
    zt = jnp.dot(xn, wt_ref[...], preferred_element_type=F32)
    ik32_ref[...] = zt[:, :IDX_DIM]
    ikk_ref[...] = zt[:, :2 * IDX_DIM].astype(BF16)
    iw_ref[...] = zt[:, 2 * IDX_DIM:2 * IDX_DIM + IDX_HEADS] * IDX_HEADS ** -0.5


def _project(x, g, w_main, w_tail, tm):
    m, d = x.shape
    row = lambda w, dt: jax.ShapeDtypeStruct((m, w), dt)
    rspec = lambda w: pl.BlockSpec((tm, w), lambda i: (i, 0))
    heads = WIDTH // HEAD_W
    th_spec = pl.BlockSpec((tm * heads, HEAD_W), lambda i: (i, 0))
    th_shape = jax.ShapeDtypeStruct((m * heads, HEAD_W), F32)
    outs = [(WIDTH, BF16),
            (WIDTH, F32), (WIDTH, BF16), (WIDTH, F32), (WIDTH, BF16), (WIDTH, F32),
            (WIDTH, BF16),
            (WIDTH, F32), (WIDTH, BF16), (WIDTH, F32), (WIDTH, BF16), (WIDTH, F32),
            (WIDTH, BF16),
            (IDX_DIM, F32), (2 * IDX_DIM, BF16), (IDX_HEADS, F32)]
    return pl.pallas_call(
        _proj_kernel,
        grid=(m // tm,),
        in_specs=[rspec(d),
                  pl.BlockSpec((1, d), lambda i: (0, 0)),
                  pl.BlockSpec(w_main.shape, lambda i: (0, 0)),
                  pl.BlockSpec(w_tail.shape, lambda i: (0, 0))],
        out_specs=[th_spec if i in TOKEN_HEAD_OUTS else rspec(w) for i, (w, _) in enumerate(outs)],
        out_shape=[th_shape if i in TOKEN_HEAD_OUTS else row(w, dt) for i, (w, dt) in enumerate(outs)],
        compiler_params=pltpu.CompilerParams(dimension_semantics=("arbitrary",),
                                             vmem_limit_bytes=VMEM_LIMIT),
        name="in_proj",
    )(x, g, w_main, w_tail)


def _diff_lambda(lp):
    a = jnp.exp(jnp.sum(lp[0:1] * lp[1:2], axis=-1, keepdims=True))
    b = jnp.exp(jnp.sum(lp[2:3] * lp[3:4], axis=-1, keepdims=True))
    return a - b + LAM_INIT


def _diff_finish(o2, lam, subln, t):
    od = o2[:t] - lam * o2[t:]
    return _rms(od, subln) * (1.0 - LAM_INIT)


def _diff_prompt_kernel(lam_ref, subln_ref, q_ref, k_ref, v_ref, o_ref, m_scr, l_scr, acc_scr, *, t, big):
    qi = pl.program_id(1)
    lam = _diff_lambda(lam_ref[...])
    r = big // t
    n_big = qi // r
    row = lax.broadcasted_iota(I32, (2 * t, HEAD_W), 0)
    row = jnp.where(row >= t, row - t, row)
    rowf = row.astype(F32)
    colf = lax.broadcasted_iota(I32, (1, big), 1).astype(F32)
    lane = lax.broadcasted_iota(I32, (t, HEAD_W), 1)
    rel = lax.broadcasted_iota(I32, (2 * t, big), 1) - jnp.tile(row, (1, big // HEAD_W))
    visible = rel <= qi * t - n_big * big
    n = big // HEAD_W

    m_scr[...] = jnp.full(m_scr.shape, -jnp.inf, F32)
    l_scr[...] = jnp.zeros(l_scr.shape, F32)
    acc_scr[...] = jnp.zeros(acc_scr.shape, F32)
    qqs = []
    for h in range(DIFF_HEADS):
        qh = q_ref[:, h * HEAD_W:(h + 1) * HEAD_W].astype(F32)
        qqs.append(jnp.concatenate([jnp.where(lane < DIFF_QK, qh, 0.0), jnp.where(lane >= DIFF_QK, qh, 0.0)],
                                   axis=0).astype(BF16))

    def tile(jb, mask):
        kb = k_ref[pl.ds(jb * r, r)].reshape(big, WIDTH)
        vb = v_ref[pl.ds(jb * r, r)].reshape(big, WIDTH)
        dist = (qi * t - jb * big).astype(F32)
        for grp in HEAD_GROUPS:
            us = {h: _nt_dot(qqs[h], kb[:, h * HEAD_W:(h + 1) * HEAD_W]) for h in grp}
            ps = {}
            for h in grp:
                slope = SLOPES_DIFF[h] * LOG2E
                u = us[h] + slope * colf
                if mask is not None:
                    u = jnp.where(mask, u, NEG_BIG)
                shift = slope * rowf + slope * dist
                part = u[:, :HEAD_W]
                for c in range(1, n):
                    part = jnp.maximum(part, u[:, c * HEAD_W:(c + 1) * HEAD_W])
                m_prev = m_scr[h]
                m_new = jnp.maximum(m_prev, jnp.max(part, axis=1, keepdims=True) - shift)
                p = jnp.exp2(u - jnp.tile(m_new + shift, (1, n)))
                alpha = jnp.exp2(m_prev - m_new)
                psum = p[:, :HEAD_W]
                for c in range(1, n):
                    psum = psum + p[:, c * HEAD_W:(c + 1) * HEAD_W]
                l_scr[h] = alpha * l_scr[h] + psum
                m_scr[h] = m_new
                ps[h] = (p.astype(BF16), alpha)
            for h in grp:
                p, alpha = ps[h]
                acc_scr[h] = alpha * acc_scr[h] + jnp.dot(p, vb[:, h * HEAD_W:(h + 1) * HEAD_W],
                                                          preferred_element_type=F32)

    def body(jb, c):
        tile(jb, None)
        return c

    lax.fori_loop(0, n_big, body, 0)
    tile(n_big, visible)
    for h in range(DIFF_HEADS):
        o2 = acc_scr[h] / jnp.sum(l_scr[h], axis=1, keepdims=True)
        o_ref[:, h * HEAD_W:(h + 1) * HEAD_W] = _diff_finish(o2, lam, subln_ref[...], t)


def _diff_prompt(lam_p, subln, dq, dk, dv, batch, seq, t):
    nt = seq // t
    k3 = dk.reshape(batch * nt, t, WIDTH)
    v3 = dv.reshape(batch * nt, t, WIDTH)
    kv_spec = pl.BlockSpec((nt, t, WIDTH), lambda b, i: (b, 0, 0), pipeline_mode=pl.Buffered(1))
    return pl.pallas_call(
        functools.partial(_diff_prompt_kernel, t=t, big=BIG),
        grid=(batch, nt),
        in_specs=[pl.BlockSpec(lam_p.shape, lambda b, i: (0, 0)),
                  pl.BlockSpec(subln.shape, lambda b, i: (0, 0)),
                  pl.BlockSpec((t, WIDTH), lambda b, i: (b * nt + i, 0)),
                  kv_spec, kv_spec],
        out_specs=pl.BlockSpec((t, WIDTH), lambda b, i: (b * nt + i, 0)),
        out_shape=jax.ShapeDtypeStruct((batch * seq, WIDTH), F32),
        scratch_shapes=[pltpu.VMEM((DIFF_HEADS, 2 * t, HEAD_W), F32)] * 3,
        compiler_params=pltpu.CompilerParams(dimension_semantics=("arbitrary", "arbitrary"),
                                             vmem_limit_bytes=VMEM_LIMIT),
        name="diff_prompt",
    )(lam_p, subln, dq, k3, v3)


def _count_keys(key_ref, n_big, r, rows, strip, pred, operands):
    tk = key_ref.shape[2]
    lane = lax.broadcasted_iota(I32, (strip, 128), 1)
    parts = []
    for s in range(rows // strip):
        ops = [o[s * strip:(s + 1) * strip] for o in operands]

        def body(jb, cnt, s=s, ops=ops):
            for i in range(r):
                for c in range(tk // 128):
                    kt = key_ref[jb * r + i, s * strip:(s + 1) * strip, c * 128:(c + 1) * 128]
                    hit = pred(kt, lane + ((jb * r + i) * tk + c * 128), *ops)
                    cnt = cnt + jnp.where(hit, 1.0, 0.0)
            return cnt

        parts.append(lax.fori_loop(0, n_big, body, jnp.zeros((strip, 128), F32)))
    return jnp.sum(jnp.concatenate(parts, axis=0), axis=1, keepdims=True)


def _select_rows(key_ref, extra_ref, n_big, r, rows, strip, ksel, idx_bits):
    tk = key_ref.shape[2]
    extra_pos = key_ref.shape[0] * tk

    def count(pred, *operands):
        tot = _count_keys(key_ref, n_big, r, rows, strip, pred, operands)
        if extra_ref is not None:
            hit = pred(extra_ref[:, :1], extra_pos, *[o[:, :1] for o in operands])
            tot = tot + jnp.where(hit, 1.0, 0.0)
        return tot

    def unsettled(n_at):
        return jnp.max(jnp.where(n_at != ksel, 1, 0))

    def value_pass(state):
        i, thr, n_at, _ = state
        cand = thr + lax.shift_left(jnp.int32(1), 31 - i)
        n_ge = count(lambda kt, pos, c: kt >= c, cand)
        ok = n_ge >= ksel
        n_at = jnp.where(ok, n_ge, n_at)
        return i + 1, jnp.where(ok, cand, thr), n_at, unsettled(n_at)

    start = (jnp.int32(0), jnp.full((rows, 128), INT_MIN, I32), jnp.full((rows, 1), 2.0 * extra_pos + 2.0, F32),
             jnp.int32(1))
    _, thr, n_at, _ = lax.while_loop(lambda st: (st[0] < 32) & (st[3] > 0), value_pass, start)

    @pl.when(jnp.max(n_at) > ksel)
    def _():
        need = ksel - count(lambda kt, pos, th: kt > th, thr)

        def pos_pass(i, p):
            cand = p + lax.shift_left(jnp.int32(1), idx_bits - 1 - i)
            n_lt = count(lambda kt, pos, th, c: (kt == th) & (pos < c), thr, cand)
            return jnp.where(n_lt < need, cand, p)

        last = lax.fori_loop(0, idx_bits, pos_pass, jnp.zeros((rows, 128), I32))
        lane = lax.broadcasted_iota(I32, (rows, 128), 1)

        def demote(j, c):
            for ch in range(tk // 128):
                kt = key_ref[j, :, ch * 128:(ch + 1) * 128]
                drop = (kt == thr) & (lane + (j * tk + ch * 128) > last)
                key_ref[j, :, ch * 128:(ch + 1) * 128] = jnp.where(drop, INT_MIN, kt)
            return c

        lax.fori_loop(0, n_big * r, demote, 0)
        if extra_ref is not None:
            e = extra_ref[...]
            extra_ref[...] = jnp.where((e == thr) & (extra_pos > last), INT_MIN, e)

    return thr


def _dsa_prompt_kernel(q_ref, k_ref, v_ref, iq_ref, ikk_ref, iw_ref, o_ref,
                       key_scr, m_scr, l_scr, acc_scr, *, t, big, sub, ksel, idx_bits):
    qi = pl.program_id(1)
    r = big // t
    n_big = qi // r + 1
    row = lax.broadcasted_iota(I32, (t, t), 0)
    col = lax.broadcasted_iota(I32, (t, t), 1)
    causal = col <= row
    lane = lax.broadcasted_iota(I32, (t, HEAD_W), 1)

    iqs, ws = [], []
    for h in range(IDX_HEADS):
        pair = iq_ref[:, (h // 2) * HEAD_W:(h // 2 + 1) * HEAD_W].astype(F32)
        keep = (lane < IDX_DIM) if h % 2 == 0 else (lane >= IDX_DIM)
        iqs.append(jnp.where(keep, pair, 0.0).astype(BF16))
        ws.append(jnp.broadcast_to(iw_ref[:, h:h + 1], (t, t)))

    def index_tile(j, masked):
        ikt = ikk_ref[j]
        acc = jnp.zeros((t, t), F32)
        for h in range(IDX_HEADS):
            acc = acc + ws[h] * jnp.maximum(_nt_dot(iqs[h], ikt), 0.0)
        if masked:
            acc = jnp.where(causal, acc, -jnp.inf)
        key_scr[j] = _score_key(acc, col + j * t)

    def index_body(j, c):
        index_tile(j, False)
        return c

    lax.fori_loop(0, qi, index_body, 0)
    index_tile(qi, True)

    def pad_body(j, c):
        key_scr[j] = jnp.full((t, t), INT_MIN, I32)
        return c

    lax.fori_loop(qi + 1, n_big * r, pad_body, 0)

    thr_rows = _select_rows(key_scr, None, n_big, r, t, sub, ksel, idx_bits)

    n = big // HEAD_W
    thr = jnp.tile(thr_rows, (1, n))
    rowf = lax.broadcasted_iota(I32, (t, HEAD_W), 0).astype(F32)
    colf = lax.broadcasted_iota(I32, (1, big), 1).astype(F32)
    rel = lax.broadcasted_iota(I32, (t, big), 1) - lax.broadcasted_iota(I32, (t, big), 0)
    visible = rel <= qi * t - (n_big - 1) * big
    m_scr[...] = jnp.full(m_scr.shape, -jnp.inf, F32)
    l_scr[...] = jnp.zeros(l_scr.shape, F32)
    acc_scr[...] = jnp.zeros(acc_scr.shape, F32)

    def attend_tile(jb, mask):
        keys = jnp.concatenate([key_scr[jb * r + i] for i in range(r)], axis=1)
        sel = keys >= thr
        if mask is not None:
            sel = sel & mask
        selbias = jnp.where(sel, 0.0, NEG_BIG)
        kb = k_ref[pl.ds(jb * r, r)].reshape(big, WIDTH)
        vb = v_ref[pl.ds(jb * r, r)].reshape(big, WIDTH)
        dist = (qi * t - jb * big).astype(F32)
        for grp in HEAD_GROUPS:
            us = {h: _nt_dot(q_ref[:, h * HEAD_W:(h + 1) * HEAD_W], kb[:, h * HEAD_W:(h + 1) * HEAD_W]) for h in grp}
            ps = {}
            for h in grp:
                slope = SLOPES_DSA[h] * LOG2E
                u = us[h] + slope * colf + selbias
                shift = slope * rowf + slope * dist
                part = u[:, :HEAD_W]
                for c in range(1, n):
                    part = jnp.maximum(part, u[:, c * HEAD_W:(c + 1) * HEAD_W])
                m_prev = m_scr[h]
                m_new = jnp.maximum(m_prev, jnp.max(part, axis=1, keepdims=True) - shift)
                p = jnp.exp2(u - jnp.tile(m_new + shift, (1, n)))
                alpha = jnp.exp2(m_prev - m_new)
                psum = p[:, :HEAD_W]
                for c in range(1, n):
                    psum = psum + p[:, c * HEAD_W:(c + 1) * HEAD_W]
                l_scr[h] = alpha * l_scr[h] + psum
                m_scr[h] = m_new
                ps[h] = (p.astype(BF16), alpha)
            for h in grp:
                p, alpha = ps[h]
                acc_scr[h] = alpha * acc_scr[h] + jnp.dot(p, vb[:, h * HEAD_W:(h + 1) * HEAD_W],
                                                          preferred_element_type=F32)

    def attend_body(jb, c):
        attend_tile(jb, None)
        return c

    lax.fori_loop(0, n_big - 1, attend_body, 0)
    attend_tile(n_big - 1, visible)
    for h in range(DSA_HEADS):
        o_ref[:, h * HEAD_W:(h + 1) * HEAD_W] = acc_scr[h] / jnp.sum(l_scr[h], axis=1, keepdims=True)


def _dsa_prompt(sq, sk, sv, iq, ikk, iw, batch, seq, t):
    nt = seq // t
    ksel = min(TOPK_MAX, seq // 4)
    assert t >= ksel and t % 128 == 0
    idx_bits = max(1, int(seq).bit_length())
    whole = lambda w: pl.BlockSpec((nt, t, w), lambda b, i: (b, 0, 0), pipeline_mode=pl.Buffered(1))
    rows = lambda w: pl.BlockSpec((t, w), lambda b, i: (b * nt + i, 0))
    return pl.pallas_call(
        functools.partial(_dsa_prompt_kernel, t=t, big=BIG, sub=SUB_ROWS, ksel=ksel, idx_bits=idx_bits),
        grid=(batch, nt),
        in_specs=[rows(WIDTH), whole(WIDTH), whole(WIDTH), rows(WIDTH), whole(2 * IDX_DIM), rows(IDX_HEADS)],
        out_specs=rows(WIDTH),
        out_shape=jax.ShapeDtypeStruct((batch * seq, WIDTH), F32),
        scratch_shapes=[pltpu.VMEM((nt, t, t), I32),
                        pltpu.VMEM((DSA_HEADS, t, HEAD_W), F32), pltpu.VMEM((DSA_HEADS, t, HEAD_W), F32),
                        pltpu.VMEM((DSA_HEADS, t, HEAD_W), F32)],
        compiler_params=pltpu.CompilerParams(dimension_semantics=("arbitrary", "arbitrary"),
                                             vmem_limit_bytes=VMEM_LIMIT),
        name="dsa_prompt",
    )(sq, sk.reshape(batch * nt, t, WIDTH), sv.reshape(batch * nt, t, WIDTH), iq,
      ikk.reshape(batch * nt, t, 2 * IDX_DIM), iw)


def _idx_sample_kernel(pt_ref, iq_ref, iw_ref, *refs, g):
    page_refs, o_ref = refs[:g], refs[g]
    pad = QROWS - IDX_HEADS
    iq = jnp.concatenate([iq_ref[0].astype(F32), jnp.zeros((pad, IDX_DIM), F32)], axis=0).astype(BF16)
    w = jnp.concatenate([iw_ref[0], jnp.zeros((pad, 1), F32)], axis=0)
    for i in range(g):
        keys = page_refs[i][0, 0].astype(BF16)
        d = jnp.dot(iq, keys, preferred_element_type=F32)
        page = keys.shape[1]
        o_ref[0, :, i * page:(i + 1) * page] = jnp.sum(w * jnp.maximum(d, 0.0), axis=0, keepdims=True)


def _idx_sample(page_table, iq, iw, cache_ik, g):
    nb, n_pages = page_table.shape
    page = cache_ik.shape[2]
    cache_t = jnp.swapaxes(cache_ik, 2, 3)

    def page_spec(i):
        return pl.BlockSpec((1, 1, IDX_DIM, page), lambda b, j, pt: (0, pt[b, j * g + i], 0, 0))

    return pl.pallas_call(
        functools.partial(_idx_sample_kernel, g=g),
        grid_spec=pltpu.PrefetchScalarGridSpec(
            num_scalar_prefetch=1,
            grid=(nb, n_pages // g),
            in_specs=[pl.BlockSpec((1, IDX_HEADS, IDX_DIM), lambda b, j, pt: (b, 0, 0)),
                      pl.BlockSpec((1, IDX_HEADS, 1), lambda b, j, pt: (b, 0, 0))]
                     + [page_spec(i) for i in range(g)],
            out_specs=pl.BlockSpec((1, 1, g * page), lambda b, j, pt: (b, 0, j)),
        ),
        out_shape=jax.ShapeDtypeStruct((nb, 1, n_pages * page), F32),
        compiler_params=pltpu.CompilerParams(dimension_semantics=("arbitrary", "arbitrary"),
                                             vmem_limit_bytes=VMEM_LIMIT),
        name="idx_sample",
    )(page_table, iq.reshape(nb, IDX_HEADS, IDX_DIM), iw.reshape(nb, IDX_HEADS, 1), *([cache_t] * g))


def _select_sample_kernel(isc_ref, iq_ref, ikk_ref, iw_ref, bias_ref, new_ref, key_scr, extra_scr,
                          *, tk, sub, ksel, idx_bits):
    rows, past = isc_ref.shape
    nt = past // tk
    lane_t = lax.broadcasted_iota(I32, (rows, tk), 1)
    for j in range(nt):
        key_scr[j] = _score_key(isc_ref[:, j * tk:(j + 1) * tk], lane_t + j * tk)
    lane = lax.broadcasted_iota(I32, (rows, HEAD_W), 1)
    kk = ikk_ref[...].astype(F32)
    new = jnp.zeros((rows, 1), F32)
    for h in range(IDX_HEADS):
        pair = iq_ref[:, (h // 2) * HEAD_W:(h // 2 + 1) * HEAD_W].astype(F32)
        keep = (lane < IDX_DIM) if h % 2 == 0 else (lane >= IDX_DIM)
        d = jnp.sum(jnp.where(keep, pair * kk, 0.0), axis=1, keepdims=True)
        new = new + iw_ref[:, h:h + 1] * jnp.maximum(d, 0.0)
    extra_scr[...] = jnp.broadcast_to(_score_key(new, past), extra_scr.shape)

    thr_rows = _select_rows(key_scr, extra_scr, nt, 1, rows, sub, ksel, idx_bits)
    thr = jnp.tile(thr_rows, (1, tk // 128))
    for j in range(nt):
        bias_ref[:, j * tk:(j + 1) * tk] = jnp.where(key_scr[j] >= thr, 0.0, NEG_BIG)
    new_ref[...] = jnp.where(extra_scr[...] >= thr_rows, 0.0, NEG_BIG)


def _select_sample(isc, iq, ikk, iw, tk):
    rows, past = isc.shape
    ksel = min(TOPK_MAX, (past + 1) // 4)
    idx_bits = int(past + 1).bit_length()
    full = lambda a: pl.BlockSpec(a.shape, lambda i: (0,) * a.ndim)
    return pl.pallas_call(
        functools.partial(_select_sample_kernel, tk=tk, sub=min(SUB_ROWS, rows), ksel=ksel, idx_bits=idx_bits),
        grid=(1,),
        in_specs=[full(isc), full(iq), full(ikk), full(iw)],
        out_specs=[pl.BlockSpec((rows, past), lambda i: (0, 0)), pl.BlockSpec((rows, 128), lambda i: (0, 0))],
        out_shape=[jax.ShapeDtypeStruct((rows, past), F32), jax.ShapeDtypeStruct((rows, 128), F32)],
        scratch_shapes=[pltpu.VMEM((past // tk, rows, tk), I32), pltpu.VMEM((rows, 128), I32)],
        compiler_params=pltpu.CompilerParams(dimension_semantics=("arbitrary",), vmem_limit_bytes=VMEM_LIMIT),
        name="select_sample",
    )(isc, iq, ikk, iw)


def _attn_sample_kernel(pt_ref, lam_ref, subln_ref, dq_ref, dkn_ref, dvn_ref, sq_ref, skn_ref, svn_ref,
                        bias_ref, new_ref, *refs, g, page):
    dk_refs, dv_refs, sk_refs, sv_refs = refs[:g], refs[g:2 * g], refs[2 * g:3 * g], refs[3 * g:4 * g]
    od_ref, os_ref, md_scr, ld_scr, accd_scr, ms_scr, ls_scr, accs_scr = refs[4 * g:]
    j = pl.program_id(1)
    nj = pl.num_programs(1)
    gp = g * page
    past = nj * gp
    heads = DIFF_HEADS
    row = lax.broadcasted_iota(I32, (QROWS, HEAD_W), 0)
    lane = lax.broadcasted_iota(I32, (QROWS, HEAD_W), 1)
    keep_d = ((row == 0) & (lane < DIFF_QK)) | ((row == 1) & (lane >= DIFF_QK))
    keep_s = row == 0
    kpos = lax.broadcasted_iota(I32, (QROWS, gp), 1) + j * gp
    dist = (past - kpos).astype(F32)

    def head_q(q_ref, h, keep):
        return jnp.where(keep, q_ref[0][:, h * HEAD_W:(h + 1) * HEAD_W].astype(F32), 0.0).astype(BF16)

    def head_rows(refs_, h):
        return jnp.concatenate([r[0, 0, pl.ds(h, page, stride=heads), :] for r in refs_], axis=0).astype(BF16)

    qds = [head_q(dq_ref, h, keep_d) for h in range(heads)]
    qss = [head_q(sq_ref, h, keep_s) for h in range(heads)]

    @pl.when(j == 0)
    def _():
        for h in range(heads):
            hs = slice(h * HEAD_W, (h + 1) * HEAD_W)
            sd = jnp.sum(qds[h].astype(F32) * dkn_ref[0][:, hs].astype(F32), axis=1, keepdims=True)
            md_scr[h] = jnp.broadcast_to(sd, (QROWS, HEAD_W))
            ld_scr[h] = jnp.ones((QROWS, HEAD_W), F32)
            accd_scr[h] = jnp.broadcast_to(dvn_ref[0][:, hs].astype(F32), (QROWS, HEAD_W))
            ss = jnp.sum(qss[h].astype(F32) * skn_ref[0][:, hs].astype(F32), axis=1, keepdims=True)
            ms_scr[h] = jnp.broadcast_to(ss + new_ref[0][:, :1], (QROWS, HEAD_W))
            ls_scr[h] = jnp.ones((QROWS, HEAD_W), F32)
            accs_scr[h] = jnp.broadcast_to(svn_ref[0][:, hs].astype(F32), (QROWS, HEAD_W))

    selbias = bias_ref[0]
    jobs = []
    for h in range(heads):
        jobs.append((h, qds[h], dk_refs, dv_refs, -(SLOPES_DIFF[h] * LOG2E) * dist, md_scr, ld_scr, accd_scr))
        jobs.append((h, qss[h], sk_refs, sv_refs, -(SLOPES_DSA[h] * LOG2E) * dist + selbias, ms_scr, ls_scr, accs_scr))
    scores = [_nt_dot(q, head_rows(k_refs, h)) + bias for h, q, k_refs, _, bias, _, _, _ in jobs]
    probs = []
    for (h, _, _, _, _, m_scr, l_scr, acc_scr), s in zip(jobs, scores):
        m_prev = m_scr[h]
        m_new = jnp.maximum(m_prev, jnp.max(s, axis=1, keepdims=True))
        p = jnp.exp2(s - m_new[:, :1])
        alpha = jnp.exp2(m_prev - m_new)
        l_scr[h] = alpha * l_scr[h] + jnp.sum(p, axis=1, keepdims=True)
        m_scr[h] = m_new
        probs.append((p.astype(BF16), alpha))
    for (h, _, _, v_refs, _, _, _, acc_scr), (p, alpha) in zip(jobs, probs):
        acc_scr[h] = alpha * acc_scr[h] + jnp.dot(p, head_rows(v_refs, h), preferred_element_type=F32)

    @pl.when(j == nj - 1)
    def _():
        lam = _diff_lambda(lam_ref[...])
        for h in range(heads):
            hs = slice(h * HEAD_W, (h + 1) * HEAD_W)
            od = accd_scr[h] / ld_scr[h]
            od_ref[0, :, hs] = _rms(od[0:1] - lam * od[1:2], subln_ref[...]) * (1.0 - LAM_INIT)
            os_ref[0, :, hs] = (accs_scr[h] / ls_scr[h])[0:1]


def _attn_sample(page_table, lam_p, subln, dq, dk_new, dv_new, sq, sk_new, sv_new, selbias, selnew,
                 cache_dk, cache_dv, cache_sk, cache_sv, g):
    nb, n_pages = page_table.shape
    n_pool, page, heads = cache_dk.shape[1:4]
    assert heads == DIFF_HEADS == DSA_HEADS
    gp = g * page
    flat = lambda c: c.reshape(c.shape[0], n_pool, page * heads, HEAD_W)
    vec = lambda a: a.reshape(nb, 1, a.shape[-1])

    def page_spec(i):
        return pl.BlockSpec((1, 1, page * heads, HEAD_W), lambda b, j, pt: (0, pt[b, j * g + i], 0, 0))

    per_b = lambda w: pl.BlockSpec((1, 1, w), lambda b, j, pt: (b, 0, 0))
    const = lambda a: pl.BlockSpec(a.shape, lambda b, j, pt: (0,) * a.ndim)
    pages = [page_spec(i) for i in range(g)]
    state = [pltpu.VMEM((heads, QROWS, HEAD_W), F32)] * 3
    return pl.pallas_call(
        functools.partial(_attn_sample_kernel, g=g, page=page),
        grid_spec=pltpu.PrefetchScalarGridSpec(
            num_scalar_prefetch=1,
            grid=(nb, n_pages // g),
            in_specs=[const(lam_p), const(subln)] + [per_b(WIDTH)] * 6
                     + [pl.BlockSpec((1, 1, gp), lambda b, j, pt: (b, 0, j)), per_b(128)] + pages * 4,
            out_specs=[per_b(WIDTH), per_b(WIDTH)],
            scratch_shapes=state * 2,
        ),
        out_shape=[jax.ShapeDtypeStruct((nb, 1, WIDTH), F32)] * 2,
        compiler_params=pltpu.CompilerParams(dimension_semantics=("arbitrary", "arbitrary"),
                                             vmem_limit_bytes=VMEM_LIMIT),
        name="attn_sample",
    )(page_table, lam_p, subln, vec(dq), vec(dk_new), vec(dv_new), vec(sq), vec(sk_new), vec(sv_new),
      selbias.reshape(nb, 1, -1), vec(selnew),
      *([flat(cache_dk)] * g), *([flat(cache_dv)] * g), *([flat(cache_sk)] * g), *([flat(cache_sv)] * g))


def _silu(x):
    return x / (1.0 + jnp.exp(-x))


def _residual_kernel(h_ref, od_ref, os_ref, dg_ref, sg_ref, p_ref, wo_ref, gp_ref, wg_ref, wp_ref, gf_ref, y_ref):
    a = (od_ref[...] * _silu(dg_ref[...])).astype(BF16)
    b = (os_ref[...] * _silu(sg_ref[...])).astype(BF16)
    h = h_ref[...] + jnp.dot(a, wo_ref[:WIDTH, :], preferred_element_type=F32) \
        + jnp.dot(b, wo_ref[WIDTH:, :], preferred_element_type=F32)
    gate_in = _rms(h, gp_ref[...]).astype(BF16)
    gate = 1.0 / (1.0 + jnp.exp(-jnp.dot(gate_in, wg_ref[...], preferred_element_type=F32)))
    h = h + gate * jnp.dot(p_ref[...].astype(BF16), wp_ref[...], preferred_element_type=F32)
    y_ref[...] = _rms(h, gf_ref[...])


def _residual(h, od, osp, dg, sg, p, w_out, g_ple, w_gate, w_proj, g_final, tm):
    m, d = h.shape
    rows = lambda w: pl.BlockSpec((tm, w), lambda i: (i, 0))
    const = lambda a: pl.BlockSpec(a.shape, lambda i: (0,) * a.ndim)
    return pl.pallas_call(
        _residual_kernel,
        grid=(m // tm,),
        in_specs=[rows(d), rows(WIDTH), rows(WIDTH), rows(WIDTH), rows(WIDTH), rows(p.shape[1]),
                  const(w_out), const(g_ple), const(w_gate), const(w_proj), const(g_final)],
        out_specs=rows(d),
        out_shape=jax.ShapeDtypeStruct((m, d), F32),
        compiler_params=pltpu.CompilerParams(dimension_semantics=("arbitrary",), vmem_limit_bytes=VMEM_LIMIT),
        name="residual",
    )(h, od, osp, dg, sg, p, w_out, g_ple, w_gate, w_proj, g_final)


def _row_tile(m, want):
    t = min(m, want)
    assert m % t == 0
    return t


def kernel(x_prompt, x_sample, cache_diff_k, cache_diff_v, cache_dsa_k, cache_dsa_v, cache_idx_k, page_table,
           p_prompt, p_sample, norm_mix, w_in, diff_lambda, diff_subln, w_out, norm_ple, w_ple_gate, w_ple_proj,
           norm_final):
    batch, seq, d_model = x_prompt.shape
    nb = x_sample.shape[0]
    assert norm_mix.shape[0] == 1 and x_sample.shape[1] == 1
    n_main = N_SEG * SEG
    n_tail = IDX_DIM + IDX_HEADS
    assert w_in.shape[2] == n_main + n_tail

    w = w_in[0]
    w_main = w[:, :n_main].astype(BF16)
    ik_cols = w[:, n_main:n_main + IDX_DIM]
    w_tail = jnp.concatenate(
        [ik_cols, ik_cols, w[:, n_main + IDX_DIM:], jnp.zeros((d_model, TAIL_W - 2 * IDX_DIM - IDX_HEADS), F32)],
        axis=1).astype(BF16)
    g_mix = norm_mix[0][None]
    subln = diff_subln[0][None]
    lam_p = diff_lambda[0]
    res_w = (w_out[0].astype(BF16), norm_ple[0][None], w_ple_gate[0].astype(BF16), w_ple_proj[0].astype(BF16),
             norm_final[None])

    mp = batch * seq
    xp = x_prompt.reshape(mp, d_model)
    (dq, dk32, dk16, dv32, dv16, dg, sq, sk32, sk16, sv32, sv16, sg, iq, ik32, ikk, iw) = _project(
        xp, g_mix, w_main, w_tail, _row_tile(mp, PROJ_ROWS))
    t = _row_tile(seq, Q_TILE)
    assert seq % BIG == 0 and BIG % t == 0
    od = _diff_prompt(lam_p, subln, dq, dk16, dv16, batch, seq, t)
    osp = _dsa_prompt(sq, sk16, sv16, iq, ikk, iw, batch, seq, t)
    y_prompt = _residual(xp, od, osp, dg, sg, p_prompt[0].reshape(mp, -1), *res_w, _row_tile(mp, PROJ_ROWS))

    xs = x_sample.reshape(nb, d_model)
    (sdq, sdk32, sdk16, sdv32, sdv16, sdg, ssq, ssk32, ssk16, ssv32, ssv16, ssg, siq, sik32, sikk, siw) = _project(
        xs, g_mix, w_main, w_tail, _row_tile(nb, PROJ_ROWS))
    n_pages = page_table.shape[1]
    isc = _idx_sample(page_table, siq, siw, cache_idx_k, math.gcd(n_pages, IDX_PAGES)).reshape(nb, -1)
    selbias, selnew = _select_sample(isc, siq, sikk, siw, _row_tile(isc.shape[1], SELECT_TILE))
    od_s, os_s = _attn_sample(page_table, lam_p, subln, sdq, sdk16, sdv16, ssq, ssk16, ssv16, selbias, selnew,
                              cache_diff_k, cache_diff_v, cache_dsa_k, cache_dsa_v, math.gcd(n_pages, ATTN_PAGES))
    y_sample = _residual(xs, od_s.reshape(nb, WIDTH), os_s.reshape(nb, WIDTH), sdg, ssg,
                         p_sample[0].reshape(nb, -1), *res_w, _row_tile(nb, PROJ_ROWS))

    ph = lambda a, hd: a.reshape(1, batch, seq, hd, -1)
    sh = lambda a, hd: a.reshape(1, nb, 1, hd, -1)
    return (y_prompt.reshape(batch, seq, d_model), y_sample.reshape(nb, 1, d_model),
            ph(dk32, DIFF_HEADS), ph(dv32, DIFF_HEADS), ph(sk32, DSA_HEADS), ph(sv32, DSA_HEADS),
            ik32.reshape(1, batch, seq, IDX_DIM),
            sh(sdk32, DIFF_HEADS), sh(sdv32, DIFF_HEADS), sh(ssk32, DSA_HEADS), sh(ssv32, DSA_HEADS),
            sik32.reshape(1, nb, 1, IDX_DIM))
```

```python
import functools
import math

import jax
import jax.numpy as jnp
import numpy as np
from jax import lax
from jax.experimental import pallas as pl
from jax.experimental.pallas import tpu as pltpu

F32 = jnp.float32
BF16 = jnp.bfloat16
I32 = jnp.int32

DIFF_HEADS = 4
DIFF_QK = 64
DSA_HEADS = 4
DSA_DIM = 128
IDX_HEADS = 8
IDX_DIM = 64
HEAD_W = 128
WIDTH = 512
TOPK_MAX = 256
NORM_EPS = 1e-6
SEG = 512
N_SEG = 9
TAIL_W = 256

NEG_BIG = -1e30
INT_MIN = -2 ** 31
INT_MAX = 2 ** 31 - 1
ZERO_TOP = 0x007FFFFF
LOG2E = 1.4426950408889634

VMEM_LIMIT = 56 * 1024 * 1024
PROJ_ROWS = 512
Q_TILE = 256
BIG = 1024
HEAD_GROUPS = ((0, 1, 2, 3),)
SUB_ROWS = 128
SELECT_TILE = 512
QROWS = 16
ATTN_PAGES = 8
IDX_PAGES = 32
TOKEN_HEAD_OUTS = (1, 3, 7, 9)


def _alibi_slopes():
    n = DIFF_HEADS + DSA_HEADS
    s = [2.0 ** (-8.0 * i / n) for i in range(1, n + 1)]
    return s[0::2], s[1::2]


SLOPES_DIFF, SLOPES_DSA = _alibi_slopes()
LAM_INIT = 0.8 - 0.6 * math.exp(-0.3 * 0)


def _nt_dot(a, b):
    return lax.dot_general(a, b, (((1,), (1,)), ((), ())), preferred_element_type=F32)


def _rms(x, g):
    ms = jnp.mean(x * x, axis=-1, keepdims=True)
    return x * lax.rsqrt(ms + NORM_EPS) * g


def _float_key(x):
    b = lax.bitcast_convert_type(x, I32)
    return b ^ (lax.shift_right_arithmetic(b, 31) & INT_MAX)


def _high_half(key):
    return lax.shift_right_arithmetic(key, 16).astype(jnp.int16)


def _low_half(key):
    return ((key & 0xFFFF) - 32768).astype(jnp.int16)


def _score_key(x, pos):
    return jnp.where(x == 0.0, ZERO_TOP - pos, _float_key(x))


def _store_token_head_rows(ref, z):
    heads = z.shape[1] // HEAD_W
    for h in range(heads):
        ref[pl.ds(h, z.shape[0], stride=heads), :] = z[:, h * HEAD_W:(h + 1) * HEAD_W]


def _proj_kernel(x_ref, g_ref, w_ref, wt_ref,
                 dq_ref, dk32_ref, dk16_ref, dv32_ref, dv16_ref, dg_ref,
                 sq_ref, sk32_ref, sk16_ref, sv32_ref, sv16_ref, sg_ref,
                 iq_ref, ik32_ref, ikk_ref, iw_ref):
    xn = _rms(x_ref[...], g_ref[...]).astype(BF16)

    def seg(i):
        return jnp.dot(xn, w_ref[:, i * SEG:(i + 1) * SEG], preferred_element_type=F32)

    dq_ref[...] = (seg(0) * (DIFF_QK ** -0.5 * LOG2E)).astype(BF16)
    z = seg(1); _store_token_head_rows(dk32_ref, z); dk16_ref[...] = z.astype(BF16)
    z = seg(2); _store_token_head_rows(dv32_ref, z); dv16_ref[...] = z.astype(BF16)
    dg_ref[...] = seg(3)
    sq_ref[...] = (seg(4) * (DSA_DIM ** -0.5 * LOG2E)).astype(BF16)
    z = seg(5); _store_token_head_rows(sk32_ref, z); sk16_ref[...] = z.astype(BF16)
    z = seg(6); _store_token_head_rows(sv32_ref, z); sv16_ref[...] = z.astype(BF16)
    sg_ref[...] = seg(7)
    iq_ref[...] = (seg(8) * IDX_DIM ** -0.5).astype(BF16)
    zt = jnp.dot(xn, wt_ref[...], preferred_element_type=F32)
    ik32_ref[...] = zt[:, :IDX_DIM]
    ikk_ref[...] = zt[:, :2 * IDX_DIM].astype(BF16)
    iw_ref[...] = zt[:, 2 * IDX_DIM:2 * IDX_DIM + IDX_HEADS] * IDX_HEADS ** -0.5


def _project(x, g, w_main, w_tail, tm):
    m, d = x.shape
    row = lambda w, dt: jax.ShapeDtypeStruct((m, w), dt)
    rspec = lambda w: pl.BlockSpec((tm, w), lambda i: (i, 0))
    heads = WIDTH // HEAD_W
    th_spec = pl.BlockSpec((tm * heads, HEAD_W), lambda i: (i, 0))
    th_shape = jax.ShapeDtypeStruct((m * heads, HEAD_W), F32)
    outs = [(WIDTH, BF16),
            (WIDTH, F32), (WIDTH, BF16), (WIDTH, F32), (WIDTH, BF16), (WIDTH, F32),
            (WIDTH, BF16),
            (WIDTH, F32), (WIDTH, BF16), (WIDTH, F32), (WIDTH, BF16), (WIDTH, F32),
            (WIDTH, BF16),
            (IDX_DIM, F32), (2 * IDX_DIM, BF16), (IDX_HEADS, F32)]
    return pl.pallas_call(
        _proj_kernel,
        grid=(m // tm,),
        in_specs=[rspec(d),
                  pl.BlockSpec((1, d), lambda i: (0, 0)),
                  pl.BlockSpec(w_main.shape, lambda i: (0, 0)),
                  pl.BlockSpec(w_tail.shape, lambda i: (0, 0))],
        out_specs=[th_spec if i in TOKEN_HEAD_OUTS else rspec(w) for i, (w, _) in enumerate(outs)],
        out_shape=[th_shape if i in TOKEN_HEAD_OUTS else row(w, dt) for i, (w, dt) in enumerate(outs)],
        compiler_params=pltpu.CompilerParams(dimension_semantics=("arbitrary",),
                                             vmem_limit_bytes=VMEM_LIMIT),
        name="in_proj",
    )(x, g, w_main, w_tail)


def _diff_lambda(lp):
    a = jnp.exp(jnp.sum(lp[0:1] * lp[1:2], axis=-1, keepdims=True))
    b = jnp.exp(jnp.sum(lp[2:3] * lp[3:4], axis=-1, keepdims=True))
    return a - b + LAM_INIT


def _diff_finish(o2, lam, subln, t):
    od = o2[:t] - lam * o2[t:]
    return _rms(od, subln) * (1.0 - LAM_INIT)


def _diff_prompt_kernel(lam_ref, subln_ref, q_ref, k_ref, v_ref, o_ref, m_scr, l_scr, acc_scr, *, t, big):
    qi = pl.program_id(1)
    lam = _diff_lambda(lam_ref[...])
    r = big // t
    n_big = qi // r
    row = lax.broadcasted_iota(I32, (2 * t, HEAD_W), 0)
    row = jnp.where(row >= t, row - t, row)
    rowf = row.astype(F32)
    colf = lax.broadcasted_iota(I32, (1, big), 1).astype(F32)
    lane = lax.broadcasted_iota(I32, (t, HEAD_W), 1)
    rel = lax.broadcasted_iota(I32, (2 * t, big), 1) - jnp.tile(row, (1, big // HEAD_W))
    visible = rel <= qi * t - n_big * big
    n = big // HEAD_W

    m_scr[...] = jnp.full(m_scr.shape, -jnp.inf, F32)
    l_scr[...] = jnp.zeros(l_scr.shape, F32)
    acc_scr[...] = jnp.zeros(acc_scr.shape, F32)
    qqs = []
    for h in range(DIFF_HEADS):
        qh = q_ref[:, h * HEAD_W:(h + 1) * HEAD_W].astype(F32)
        qqs.append(jnp.concatenate([jnp.where(lane < DIFF_QK, qh, 0.0), jnp.where(lane >= DIFF_QK, qh, 0.0)],
                                   axis=0).astype(BF16))

    def tile(jb, mask):
        kb = k_ref[pl.ds(jb * r, r)].reshape(big, WIDTH)
        vb = v_ref[pl.ds(jb * r, r)].reshape(big, WIDTH)
        dist = (qi * t - jb * big).astype(F32)
        for grp in HEAD_GROUPS:
            us = {h: _nt_dot(qqs[h], kb[:, h * HEAD_W:(h + 1) * HEAD_W]) for h in grp}
            ps = {}
            for h in grp:
                slope = SLOPES_DIFF[h] * LOG2E
                u = us[h] + slope * colf
                if mask is not None:
                    u = jnp.where(mask, u, NEG_BIG)
                shift = slope * rowf + slope * dist
                part = u[:, :HEAD_W]
                for c in range(1, n):
                    part = jnp.maximum(part, u[:, c * HEAD_W:(c + 1) * HEAD_W])
                m_prev = m_scr[h]
                m_new = jnp.maximum(m_prev, jnp.max(part, axis=1, keepdims=True) - shift)
                p = jnp.exp2(u - jnp.tile(m_new + shift, (1, n)))
                alpha = jnp.exp2(m_prev - m_new)
                psum = p[:, :HEAD_W]
                for c in range(1, n):
                    psum = psum + p[:, c * HEAD_W:(c + 1) * HEAD_W]
                l_scr[h] = alpha * l_scr[h] + psum
                m_scr[h] = m_new
                ps[h] = (p.astype(BF16), alpha)
            for h in grp:
                p, alpha = ps[h]
                acc_scr[h] = alpha * acc_scr[h] + jnp.dot(p, vb[:, h * HEAD_W:(h + 1) * HEAD_W],
                                                          preferred_element_type=F32)

    def body(jb, c):
        tile(jb, None)
        return c

    lax.fori_loop(0, n_big, body, 0)
    tile(n_big, visible)
    for h in range(DIFF_HEADS):
        o2 = acc_scr[h] / jnp.sum(l_scr[h], axis=1, keepdims=True)
        o_ref[:, h * HEAD_W:(h + 1) * HEAD_W] = _diff_finish(o2, lam, subln_ref[...], t)


def _diff_prompt(lam_p, subln, dq, dk, dv, batch, seq, t):
    nt = seq // t
    k3 = dk.reshape(batch * nt, t, WIDTH)
    v3 = dv.reshape(batch * nt, t, WIDTH)
    kv_spec = pl.BlockSpec((nt, t, WIDTH), lambda b, i: (b, 0, 0), pipeline_mode=pl.Buffered(1))
    return pl.pallas_call(
        functools.partial(_diff_prompt_kernel, t=t, big=BIG),
        grid=(batch, nt),
        in_specs=[pl.BlockSpec(lam_p.shape, lambda b, i: (0, 0)),
                  pl.BlockSpec(subln.shape, lambda b, i: (0, 0)),
                  pl.BlockSpec((t, WIDTH), lambda b, i: (b * nt + i, 0)),
                  kv_spec, kv_spec],
        out_specs=pl.BlockSpec((t, WIDTH), lambda b, i: (b * nt + i, 0)),
        out_shape=jax.ShapeDtypeStruct((batch * seq, WIDTH), F32),
        scratch_shapes=[pltpu.VMEM((DIFF_HEADS, 2 * t, HEAD_W), F32)] * 3,
        compiler_params=pltpu.CompilerParams(dimension_semantics=("arbitrary", "arbitrary"),
                                             vmem_limit_bytes=VMEM_LIMIT),
        name="diff_prompt",
    )(lam_p, subln, dq, k3, v3)


def _count_keys(key_ref, n_big, r, rows, strip, pred, operands):
    tk = key_ref.shape[2]
    dt = key_ref.dtype
    one, zero = (1.0, 0.0) if dt == I32 else (jnp.int16(1), jnp.int16(0))
    lane = lax.broadcasted_iota(I32, (strip, 128), 1)
    parts = []
    for s in range(rows // strip):
        ops = [o[s * strip:(s + 1) * strip] for o in operands]

        def body(jb, cnt, s=s, ops=ops):
            for i in range(r):
                for c in range(tk // 128):
                    kt = key_ref[jb * r + i, s * strip:(s + 1) * strip, c * 128:(c + 1) * 128]
                    hit = pred(kt, lane + ((jb * r + i) * tk + c * 128), *ops)
                    cnt = cnt + jnp.where(hit, one, zero)
            return cnt

        parts.append(lax.fori_loop(0, n_big, body, jnp.zeros((strip, 128), F32 if dt == I32 else dt)))
    return jnp.sum(jnp.concatenate(parts, axis=0).astype(F32), axis=1, keepdims=True)


def _select_rows(key_ref, hi_ref, extra_ref, n_big, r, rows, strip, ksel, idx_bits):
    tk = key_ref.shape[2]
    extra_pos = key_ref.shape[0] * tk

    def count(pred, *operands):
        tot = _count_keys(key_ref, n_big, r, rows, strip, pred, operands)
        if extra_ref is not None:
            hit = pred(extra_ref[:, :1], extra_pos, *[o[:, :1] for o in operands])
            tot = tot + jnp.where(hit, 1.0, 0.0)
        return tot

    def unsettled(n_at):
        return jnp.max(jnp.where(n_at != ksel, 1, 0))

    def accept(cand, n_ge, thr, n_at):
        ok = n_ge >= ksel
        return jnp.where(ok, cand, thr), jnp.where(ok, n_ge, n_at)

    def high_pass(i, state):
        thr, n_at = state
        cand = thr + lax.shift_left(jnp.int32(1), 31 - i)
        c16 = lax.shift_right_arithmetic(cand, 16).astype(jnp.int16)
        n_ge = _count_keys(hi_ref, n_big, r, rows, strip, lambda kt, pos, c: kt >= c, (c16,))
        if extra_ref is not None:
            n_ge = n_ge + jnp.where(extra_ref[:, :1] >= cand[:, :1], 1.0, 0.0)
        return accept(cand, n_ge, thr, n_at)

    def low_pass(state):
        i, thr, n_at, _ = state
        cand = thr + lax.shift_left(jnp.int32(1), 31 - i)
        n_ge = _count_keys(hi_ref, n_big, r, rows, strip, lambda kt, pos, c: kt >= c, (_low_half(cand),))
        if extra_ref is not None:
            n_ge = n_ge + jnp.where(extra_ref[:, :1] >= cand[:, :1], 1.0, 0.0)
        thr, n_at = accept(cand, n_ge, thr, n_at)
        return i + 1, thr, n_at, unsettled(n_at)

    thr, n_at = lax.fori_loop(0, 16, high_pass, (jnp.full((rows, 128), INT_MIN, I32),
                                                 jnp.full((rows, 1), 2.0 * extra_pos + 2.0, F32)))

    t16 = _high_half(thr)

    def remap(j, c):
        for s in range(rows // strip):
            rs = slice(s * strip, (s + 1) * strip)
            for ch in range(tk // 128):
                cs = slice(ch * 128, (ch + 1) * 128)
                h = hi_ref[j, rs, cs]
                side = jnp.where(h > t16[rs], jnp.int16(32767), jnp.int16(-32768))
                hi_ref[j, rs, cs] = jnp.where(h == t16[rs], _low_half(key_ref[j, rs, cs]), side)
        return c

    lax.fori_loop(0, n_big * r, remap, 0)
    _, thr, n_at, _ = lax.while_loop(lambda st: (st[0] < 32) & (st[3] > 0), low_pass,
                                     (jnp.int32(16), thr, n_at, unsettled(n_at)))

    @pl.when(jnp.max(n_at) > ksel)
    def _():
        need = ksel - count(lambda kt, pos, th: kt > th, thr)

        def pos_pass(i, p):
            cand = p + lax.shift_left(jnp.int32(1), idx_bits - 1 - i)
            n_lt = count(lambda kt, pos, th, c: (kt == th) & (pos < c), thr, cand)
            return jnp.where(n_lt < need, cand, p)

        last = lax.fori_loop(0, idx_bits, pos_pass, jnp.zeros((rows, 128), I32))
        lane = lax.broadcasted_iota(I32, (rows, 128), 1)

        def demote(j, c):
            for ch in range(tk // 128):
                kt = key_ref[j, :, ch * 128:(ch + 1) * 128]
                drop = (kt == thr) & (lane + (j * tk + ch * 128) > last)
                key_ref[j, :, ch * 128:(ch + 1) * 128] = jnp.where(drop, INT_MIN, kt)
            return c

        lax.fori_loop(0, n_big * r, demote, 0)
        if extra_ref is not None:
            e = extra_ref[...]
            extra_ref[...] = jnp.where((e == thr) & (extra_pos > last), INT_MIN, e)

    return thr


def _dsa_prompt_kernel(q_ref, k_ref, v_ref, iq_ref, ikk_ref, iw_ref, o_ref,
                       key_scr, hi_scr, m_scr, l_scr, acc_scr, *, t, big, sub, ksel, idx_bits):
    qi = pl.program_id(1)
    r = big // t
    n_big = qi // r + 1
    row = lax.broadcasted_iota(I32, (t, t), 0)
    col = lax.broadcasted_iota(I32, (t, t), 1)
    causal = col <= row
    lane = lax.broadcasted_iota(I32, (t, HEAD_W), 1)

    iqs, ws = [], []
    for h in range(IDX_HEADS):
        pair = iq_ref[:, (h // 2) * HEAD_W:(h // 2 + 1) * HEAD_W].astype(F32)
        keep = (lane < IDX_DIM) if h % 2 == 0 else (lane >= IDX_DIM)
        iqs.append(jnp.where(keep, pair, 0.0).astype(BF16))
        ws.append(jnp.broadcast_to(iw_ref[:, h:h + 1], (t, t)))

    def index_tile(j, masked):
        ikt = ikk_ref[j]
        acc = jnp.zeros((t, t), F32)
        for h in range(IDX_HEADS):
            acc = acc + ws[h] * jnp.maximum(_nt_dot(iqs[h], ikt), 0.0)
        if masked:
            acc = jnp.where(causal, acc, -jnp.inf)
        key = _score_key(acc, col + j * t)
        key_scr[j] = key
        hi_scr[j] = _high_half(key)

    def index_body(j, c):
        index_tile(j, False)
        return c

    lax.fori_loop(0, qi, index_body, 0)
    index_tile(qi, True)

    def pad_body(j, c):
        key_scr[j] = jnp.full((t, t), INT_MIN, I32)
        hi_scr[j] = jnp.full((t, t), INT_MIN >> 16, jnp.int16)
        return c

    lax.fori_loop(qi + 1, n_big * r, pad_body, 0)

    thr_rows = _select_rows(key_scr, hi_scr, None, n_big, r, t, sub, ksel, idx_bits)

    n = big // HEAD_W
    thr = jnp.tile(thr_rows, (1, n))
    rowf = lax.broadcasted_iota(I32, (t, HEAD_W), 0).astype(F32)
    colf = lax.broadcasted_iota(I32, (1, big), 1).astype(F32)
    rel = lax.broadcasted_iota(I32, (t, big), 1) - lax.broadcasted_iota(I32, (t, big), 0)
    visible = rel <= qi * t - (n_big - 1) * big
    m_scr[...] = jnp.full(m_scr.shape, -jnp.inf, F32)
    l_scr[...] = jnp.zeros(l_scr.shape, F32)
    acc_scr[...] = jnp.zeros(acc_scr.shape, F32)

    def attend_tile(jb, mask):
        keys = jnp.concatenate([key_scr[jb * r + i] for i in range(r)], axis=1)
        sel = keys >= thr
        if mask is not None:
            sel = sel & mask
        selbias = jnp.where(sel, 0.0, NEG_BIG)
        kb = k_ref[pl.ds(jb * r, r)].reshape(big, WIDTH)
        vb = v_ref[pl.ds(jb * r, r)].reshape(big, WIDTH)
        dist = (qi * t - jb * big).astype(F32)
        for grp in HEAD_GROUPS:
            us = {h: _nt_dot(q_ref[:, h * HEAD_W:(h + 1) * HEAD_W], kb[:, h * HEAD_W:(h + 1) * HEAD_W]) for h in grp}
            ps = {}
            for h in grp:
                slope = SLOPES_DSA[h] * LOG2E
                u = us[h] + slope * colf + selbias
                shift = slope * rowf + slope * dist
                part = u[:, :HEAD_W]
                for c in range(1, n):
                    part = jnp.maximum(part, u[:, c * HEAD_W:(c + 1) * HEAD_W])
                m_prev = m_scr[h]
                m_new = jnp.maximum(m_prev, jnp.max(part, axis=1, keepdims=True) - shift)
                p = jnp.exp2(u - jnp.tile(m_new + shift, (1, n)))
                alpha = jnp.exp2(m_prev - m_new)
                psum = p[:, :HEAD_W]
                for c in range(1, n):
                    psum = psum + p[:, c * HEAD_W:(c + 1) * HEAD_W]
                l_scr[h] = alpha * l_scr[h] + psum
                m_scr[h] = m_new
                ps[h] = (p.astype(BF16), alpha)
            for h in grp:
                p, alpha = ps[h]
                acc_scr[h] = alpha * acc_scr[h] + jnp.dot(p, vb[:, h * HEAD_W:(h + 1) * HEAD_W],
                                                          preferred_element_type=F32)

    def attend_body(jb, c):
        attend_tile(jb, None)
        return c

    lax.fori_loop(0, n_big - 1, attend_body, 0)
    attend_tile(n_big - 1, visible)
    for h in range(DSA_HEADS):
        o_ref[:, h * HEAD_W:(h + 1) * HEAD_W] = acc_scr[h] / jnp.sum(l_scr[h], axis=1, keepdims=True)


def _dsa_prompt(sq, sk, sv, iq, ikk, iw, batch, seq, t):
    nt = seq // t
    ksel = min(TOPK_MAX, seq // 4)
    assert t >= ksel and t % 128 == 0
    idx_bits = max(1, int(seq).bit_length())
    whole = lambda w: pl.BlockSpec((nt, t, w), lambda b, i: (b, 0, 0), pipeline_mode=pl.Buffered(1))
    rows = lambda w: pl.BlockSpec((t, w), lambda b, i: (b * nt + i, 0))
    return pl.pallas_call(
        functools.partial(_dsa_prompt_kernel, t=t, big=BIG, sub=SUB_ROWS, ksel=ksel, idx_bits=idx_bits),
        grid=(batch, nt),
        in_specs=[rows(WIDTH), whole(WIDTH), whole(WIDTH), rows(WIDTH), whole(2 * IDX_DIM), rows(IDX_HEADS)],
        out_specs=rows(WIDTH),
        out_shape=jax.ShapeDtypeStruct((batch * seq, WIDTH), F32),
        scratch_shapes=[pltpu.VMEM((nt, t, t), I32), pltpu.VMEM((nt, t, t), jnp.int16),
                        pltpu.VMEM((DSA_HEADS, t, HEAD_W), F32), pltpu.VMEM((DSA_HEADS, t, HEAD_W), F32),
                        pltpu.VMEM((DSA_HEADS, t, HEAD_W), F32)],
        compiler_params=pltpu.CompilerParams(dimension_semantics=("arbitrary", "arbitrary"),
                                             vmem_limit_bytes=VMEM_LIMIT),
        name="dsa_prompt",
    )(sq, sk.reshape(batch * nt, t, WIDTH), sv.reshape(batch * nt, t, WIDTH), iq,
      ikk.reshape(batch * nt, t, 2 * IDX_DIM), iw)


def _idx_sample_kernel(pt_ref, iq_ref, iw_ref, *refs, g):
    page_refs, o_ref = refs[:g], refs[g]
    pad = QROWS - IDX_HEADS
    iq = jnp.concatenate([iq_ref[0].astype(F32), jnp.zeros((pad, IDX_DIM), F32)], axis=0).astype(BF16)
    w = jnp.concatenate([iw_ref[0], jnp.zeros((pad, 1), F32)], axis=0)
    for i in range(g):
        keys = page_refs[i][0, 0].astype(BF16)
        d = jnp.dot(iq, keys, preferred_element_type=F32)
        page = keys.shape[1]
        o_ref[0, :, i * page:(i + 1) * page] = jnp.sum(w * jnp.maximum(d, 0.0), axis=0, keepdims=True)


def _idx_sample(page_table, iq, iw, cache_ik, g):
    nb, n_pages = page_table.shape
    page = cache_ik.shape[2]
    cache_t = jnp.swapaxes(cache_ik, 2, 3)

    def page_spec(i):
        return pl.BlockSpec((1, 1, IDX_DIM, page), lambda b, j, pt: (0, pt[b, j * g + i], 0, 0))

    return pl.pallas_call(
        functools.partial(_idx_sample_kernel, g=g),
        grid_spec=pltpu.PrefetchScalarGridSpec(
            num_scalar_prefetch=1,
            grid=(nb, n_pages // g),
            in_specs=[pl.BlockSpec((1, IDX_HEADS, IDX_DIM), lambda b, j, pt: (b, 0, 0)),
                      pl.BlockSpec((1, IDX_HEADS, 1), lambda b, j, pt: (b, 0, 0))]
                     + [page_spec(i) for i in range(g)],
            out_specs=pl.BlockSpec((1, 1, g * page), lambda b, j, pt: (b, 0, j)),
        ),
        out_shape=jax.ShapeDtypeStruct((nb, 1, n_pages * page), F32),
        compiler_params=pltpu.CompilerParams(dimension_semantics=("arbitrary", "arbitrary"),
                                             vmem_limit_bytes=VMEM_LIMIT),
        name="idx_sample",
    )(page_table, iq.reshape(nb, IDX_HEADS, IDX_DIM), iw.reshape(nb, IDX_HEADS, 1), *([cache_t] * g))


def _select_sample_kernel(isc_ref, iq_ref, ikk_ref, iw_ref, bias_ref, new_ref, key_scr, hi_scr, extra_scr,
                          *, tk, sub, ksel, idx_bits):
    rows, past = isc_ref.shape
    nt = past // tk
    lane_t = lax.broadcasted_iota(I32, (rows, tk), 1)
    for j in range(nt):
        key = _score_key(isc_ref[:, j * tk:(j + 1) * tk], lane_t + j * tk)
        key_scr[j] = key
        hi_scr[j] = _high_half(key)
    lane = lax.broadcasted_iota(I32, (rows, HEAD_W), 1)
    kk = ikk_ref[...].astype(F32)
    new = jnp.zeros((rows, 1), F32)
    for h in range(IDX_HEADS):
        pair = iq_ref[:, (h // 2) * HEAD_W:(h // 2 + 1) * HEAD_W].astype(F32)
        keep = (lane < IDX_DIM) if h % 2 == 0 else (lane >= IDX_DIM)
        d = jnp.sum(jnp.where(keep, pair * kk, 0.0), axis=1, keepdims=True)
        new = new + iw_ref[:, h:h + 1] * jnp.maximum(d, 0.0)
    extra_scr[...] = jnp.broadcast_to(_score_key(new, past), extra_scr.shape)

    thr_rows = _select_rows(key_scr, hi_scr, extra_scr, nt, 1, rows, sub, ksel, idx_bits)
    thr = jnp.tile(thr_rows, (1, tk // 128))
    for j in range(nt):
        bias_ref[:, j * tk:(j + 1) * tk] = jnp.where(key_scr[j] >= thr, 0.0, NEG_BIG)
    new_ref[...] = jnp.where(extra_scr[...] >= thr_rows, 0.0, NEG_BIG)


def _select_sample(isc, iq, ikk, iw, tk):
    rows, past = isc.shape
    ksel = min(TOPK_MAX, (past + 1) // 4)
    idx_bits = int(past + 1).bit_length()
    full = lambda a: pl.BlockSpec(a.shape, lambda i: (0,) * a.ndim)
    return pl.pallas_call(
        functools.partial(_select_sample_kernel, tk=tk, sub=min(SUB_ROWS, rows), ksel=ksel, idx_bits=idx_bits),
        grid=(1,),
        in_specs=[full(isc), full(iq), full(ikk), full(iw)],
        out_specs=[pl.BlockSpec((rows, past), lambda i: (0, 0)), pl.BlockSpec((rows, 128), lambda i: (0, 0))],
        out_shape=[jax.ShapeDtypeStruct((rows, past), F32), jax.ShapeDtypeStruct((rows, 128), F32)],
        scratch_shapes=[pltpu.VMEM((past // tk, rows, tk), I32), pltpu.VMEM((past // tk, rows, tk), jnp.int16),
                        pltpu.VMEM((rows, 128), I32)],
        compiler_params=pltpu.CompilerParams(dimension_semantics=("arbitrary",), vmem_limit_bytes=VMEM_LIMIT),
        name="select_sample",
    )(isc, iq, ikk, iw)


def _attn_sample_kernel(pt_ref, lam_ref, subln_ref, dq_ref, dkn_ref, dvn_ref, sq_ref, skn_ref, svn_ref,
                        bias_ref, new_ref, *refs, g, page):
    dk_refs, dv_refs, sk_refs, sv_refs = refs[:g], refs[g:2 * g], refs[2 * g:3 * g], refs[3 * g:4 * g]
    od_ref, os_ref, md_scr, ld_scr, accd_scr, ms_scr, ls_scr, accs_scr = refs[4 * g:]
    j = pl.program_id(1)
    nj = pl.num_programs(1)
    gp = g * page
    past = nj * gp
    heads = DIFF_HEADS
    row = lax.broadcasted_iota(I32, (QROWS, HEAD_W), 0)
    lane = lax.broadcasted_iota(I32, (QROWS, HEAD_W), 1)
    keep_d = ((row == 0) & (lane < DIFF_QK)) | ((row == 1) & (lane >= DIFF_QK))
    keep_s = row == 0
    kpos = lax.broadcasted_iota(I32, (QROWS, gp), 1) + j * gp
    dist = (past - kpos).astype(F32)

    def head_q(q_ref, h, keep):
        return jnp.where(keep, q_ref[0][:, h * HEAD_W:(h + 1) * HEAD_W].astype(F32), 0.0).astype(BF16)

    def head_rows(refs_, h):
        return jnp.concatenate([r[0, 0, pl.ds(h, page, stride=heads), :] for r in refs_], axis=0).astype(BF16)

    qds = [head_q(dq_ref, h, keep_d) for h in range(heads)]
    qss = [head_q(sq_ref, h, keep_s) for h in range(heads)]

    @pl.when(j == 0)
    def _():
        for h in range(heads):
            hs = slice(h * HEAD_W, (h + 1) * HEAD_W)
            sd = jnp.sum(qds[h].astype(F32) * dkn_ref[0][:, hs].astype(F32), axis=1, keepdims=True)
            md_scr[h] = jnp.broadcast_to(sd, (QROWS, HEAD_W))
            ld_scr[h] = jnp.ones((QROWS, HEAD_W), F32)
            accd_scr[h] = jnp.broadcast_to(dvn_ref[0][:, hs].astype(F32), (QROWS, HEAD_W))
            ss = jnp.sum(qss[h].astype(F32) * skn_ref[0][:, hs].astype(F32), axis=1, keepdims=True)
            ms_scr[h] = jnp.broadcast_to(ss + new_ref[0][:, :1], (QROWS, HEAD_W))
            ls_scr[h] = jnp.ones((QROWS, HEAD_W), F32)
            accs_scr[h] = jnp.broadcast_to(svn_ref[0][:, hs].astype(F32), (QROWS, HEAD_W))

    selbias = bias_ref[0]
    jobs = []
    for h in range(heads):
        jobs.append((h, qds[h], dk_refs, dv_refs, -(SLOPES_DIFF[h] * LOG2E) * dist, md_scr, ld_scr, accd_scr))
        jobs.append((h, qss[h], sk_refs, sv_refs, -(SLOPES_DSA[h] * LOG2E) * dist + selbias, ms_scr, ls_scr, accs_scr))
    scores = [_nt_dot(q, head_rows(k_refs, h)) + bias for h, q, k_refs, _, bias, _, _, _ in jobs]
    probs = []
    for (h, _, _, _, _, m_scr, l_scr, acc_scr), s in zip(jobs, scores):
        m_prev = m_scr[h]
        m_new = jnp.maximum(m_prev, jnp.max(s, axis=1, keepdims=True))
        p = jnp.exp2(s - m_new[:, :1])
        alpha = jnp.exp2(m_prev - m_new)
        l_scr[h] = alpha * l_scr[h] + jnp.sum(p, axis=1, keepdims=True)
        m_scr[h] = m_new
        probs.append((p.astype(BF16), alpha))
    for (h, _, _, v_refs, _, _, _, acc_scr), (p, alpha) in zip(jobs, probs):
        acc_scr[h] = alpha * acc_scr[h] + jnp.dot(p, head_rows(v_refs, h), preferred_element_type=F32)

    @pl.when(j == nj - 1)
    def _():
        lam = _diff_lambda(lam_ref[...])
        for h in range(heads):
            hs = slice(h * HEAD_W, (h + 1) * HEAD_W)
            od = accd_scr[h] / ld_scr[h]
            od_ref[0, :, hs] = _rms(od[0:1] - lam * od[1:2], subln_ref[...]) * (1.0 - LAM_INIT)
            os_ref[0, :, hs] = (accs_scr[h] / ls_scr[h])[0:1]


def _attn_sample(page_table, lam_p, subln, dq, dk_new, dv_new, sq, sk_new, sv_new, selbias, selnew,
                 cache_dk, cache_dv, cache_sk, cache_sv, g):
    nb, n_pages = page_table.shape
    n_pool, page, heads = cache_dk.shape[1:4]
    assert heads == DIFF_HEADS == DSA_HEADS
    gp = g * page
    flat = lambda c: c.reshape(c.shape[0], n_pool, page * heads, HEAD_W)
    vec = lambda a: a.reshape(nb, 1, a.shape[-1])

    def page_spec(i):
        return pl.BlockSpec((1, 1, page * heads, HEAD_W), lambda b, j, pt: (0, pt[b, j * g + i], 0, 0))

    per_b = lambda w: pl.BlockSpec((1, 1, w), lambda b, j, pt: (b, 0, 0))
    const = lambda a: pl.BlockSpec(a.shape, lambda b, j, pt: (0,) * a.ndim)
    pages = [page_spec(i) for i in range(g)]
    state = [pltpu.VMEM((heads, QROWS, HEAD_W), F32)] * 3
    return pl.pallas_call(
        functools.partial(_attn_sample_kernel, g=g, page=page),
        grid_spec=pltpu.PrefetchScalarGridSpec(
            num_scalar_prefetch=1,
            grid=(nb, n_pages // g),
            in_specs=[const(lam_p), const(subln)] + [per_b(WIDTH)] * 6
                     + [pl.BlockSpec((1, 1, gp), lambda b, j, pt: (b, 0, j)), per_b(128)] + pages * 4,
            out_specs=[per_b(WIDTH), per_b(WIDTH)],
            scratch_shapes=state * 2,
        ),
        out_shape=[jax.ShapeDtypeStruct((nb, 1, WIDTH), F32)] * 2,
        compiler_params=pltpu.CompilerParams(dimension_semantics=("arbitrary", "arbitrary"),
                                             vmem_limit_bytes=VMEM_LIMIT),
        name="attn_sample",
    )(page_table, lam_p, subln, vec(dq), vec(dk_new), vec(dv_new), vec(sq), vec(sk_new), vec(sv_new),
      selbias.reshape(nb, 1, -1), vec(selnew),
      *([flat(cache_dk)] * g), *([flat(cache_dv)] * g), *([flat(cache_sk)] * g), *([flat(cache_sv)] * g))


def _silu(x):
    return x / (1.0 + jnp.exp(-x))


def _residual_kernel(h_ref, od_ref, os_ref, dg_ref, sg_ref, p_ref, wo_ref, gp_ref, wg_ref, wp_ref, gf_ref, y_ref):
    a = (od_ref[...] * _silu(dg_ref[...])).astype(BF16)
    b = (os_ref[...] * _silu(sg_ref[...])).astype(BF16)
    h = h_ref[...] + jnp.dot(a, wo_ref[:WIDTH, :], preferred_element_type=F32) \
        + jnp.dot(b, wo_ref[WIDTH:, :], preferred_element_type=F32)
    gate_in = _rms(h, gp_ref[...]).astype(BF16)
    gate = 1.0 / (1.0 + jnp.exp(-jnp.dot(gate_in, wg_ref[...], preferred_element_type=F32)))
    h = h + gate * jnp.dot(p_ref[...].astype(BF16), wp_ref[...], preferred_element_type=F32)
    y_ref[...] = _rms(h, gf_ref[...])


def _residual(h, od, osp, dg, sg, p, w_out, g_ple, w_gate, w_proj, g_final, tm):
    m, d = h.shape
    rows = lambda w: pl.BlockSpec((tm, w), lambda i: (i, 0))
    const = lambda a: pl.BlockSpec(a.shape, lambda i: (0,) * a.ndim)
    return pl.pallas_call(
        _residual_kernel,
        grid=(m // tm,),
        in_specs=[rows(d), rows(WIDTH), rows(WIDTH), rows(WIDTH), rows(WIDTH), rows(p.shape[1]),
                  const(w_out), const(g_ple), const(w_gate), const(w_proj), const(g_final)],
        out_specs=rows(d),
        out_shape=jax.ShapeDtypeStruct((m, d), F32),
        compiler_params=pltpu.CompilerParams(dimension_semantics=("arbitrary",), vmem_limit_bytes=VMEM_LIMIT),
        name="residual",
    )(h, od, osp, dg, sg, p, w_out, g_ple, w_gate, w_proj, g_final)


def _row_tile(m, want):
    t = min(m, want)
    assert m % t == 0
    return t


def kernel(x_prompt, x_sample, cache_diff_k, cache_diff_v, cache_dsa_k, cache_dsa_v, cache_idx_k, page_table,
           p_prompt, p_sample, norm_mix, w_in, diff_lambda, diff_subln, w_out, norm_ple, w_ple_gate, w_ple_proj,
           norm_final):
    batch, seq, d_model = x_prompt.shape
    nb = x_sample.shape[0]
    assert norm_mix.shape[0] == 1 and x_sample.shape[1] == 1
    n_main = N_SEG * SEG
    n_tail = IDX_DIM + IDX_HEADS
    assert w_in.shape[2] == n_main + n_tail

    w = w_in[0]
    w_main = w[:, :n_main].astype(BF16)
    ik_cols = w[:, n_main:n_main + IDX_DIM]
    w_tail = jnp.concatenate(
        [ik_cols, ik_cols, w[:, n_main + IDX_DIM:], jnp.zeros((d_model, TAIL_W - 2 * IDX_DIM - IDX_HEADS), F32)],
        axis=1).astype(BF16)
    g_mix = norm_mix[0][None]
    subln = diff_subln[0][None]
    lam_p = diff_lambda[0]
    res_w = (w_out[0].astype(BF16), norm_ple[0][None], w_ple_gate[0].astype(BF16), w_ple_proj[0].astype(BF16),
             norm_final[None])

    mp = batch * seq
    xp = x_prompt.reshape(mp, d_model)
    (dq, dk32, dk16, dv32, dv16, dg, sq, sk32, sk16, sv32, sv16, sg, iq, ik32, ikk, iw) = _project(
        xp, g_mix, w_main, w_tail, _row_tile(mp, PROJ_ROWS))
    t = _row_tile(seq, Q_TILE)
    assert seq % BIG == 0 and BIG % t == 0
    od = _diff_prompt(lam_p, subln, dq, dk16, dv16, batch, seq, t)
    osp = _dsa_prompt(sq, sk16, sv16, iq, ikk, iw, batch, seq, t)
    y_prompt = _residual(xp, od, osp, dg, sg, p_prompt[0].reshape(mp, -1), *res_w, _row_tile(mp, PROJ_ROWS))

    xs = x_sample.reshape(nb, d_model)
    (sdq, sdk32, sdk16, sdv32, sdv16, sdg, ssq, ssk32, ssk16, ssv32, ssv16, ssg, siq, sik32, sikk, siw) = _project(
        xs, g_mix, w_main, w_tail, _row_tile(nb, PROJ_ROWS))
    n_pages = page_table.shape[1]
    isc = _idx_sample(page_table, siq, siw, cache_idx_k, math.gcd(n_pages, IDX_PAGES)).reshape(nb, -1)
    selbias, selnew = _select_sample(isc, siq, sikk, siw, _row_tile(isc.shape[1], SELECT_TILE))
    od_s, os_s = _attn_sample(page_table, lam_p, subln, sdq, sdk16, sdv16, ssq, ssk16, ssv16, selbias, selnew,
                              cache_diff_k, cache_diff_v, cache_dsa_k, cache_dsa_v, math.gcd(n_pages, ATTN_PAGES))
    y_sample = _residual(xs, od_s.reshape(nb, WIDTH), os_s.reshape(nb, WIDTH), sdg, ssg,
                         p_sample[0].reshape(nb, -1), *res_w, _row_tile(nb, PROJ_ROWS))

    ph = lambda a, hd: a.reshape(1, batch, seq, hd, -1)
    sh = lambda a, hd: a.reshape(1, nb, 1, hd, -1)
    return (y_prompt.reshape(batch, seq, d_model), y_sample.reshape(nb, 1, d_model),
            ph(dk32, DIFF_HEADS), ph(dv32, DIFF_HEADS), ph(sk32, DSA_HEADS), ph(sv32, DSA_HEADS),
            ik32.reshape(1, batch, seq, IDX_DIM),
            sh(sdk32, DIFF_HEADS), sh(sdv32, DIFF_HEADS), sh(ssk32, DSA_HEADS), sh(ssv32, DSA_HEADS),
            sik32.reshape(1, nb, 1, IDX_DIM))
```

```python
import functools
import math

import jax
import jax.numpy as jnp
import numpy as np
from jax import lax
from jax.experimental import pallas as pl
from jax.experimental.pallas import tpu as pltpu

F32 = jnp.float32
BF16 = jnp.bfloat16
I32 = jnp.int32

DIFF_HEADS = 4
DIFF_QK = 64
DSA_HEADS = 4
DSA_DIM = 128
IDX_HEADS = 8
IDX_DIM = 64
HEAD_W = 128
WIDTH = 512
TOPK_MAX = 256
NORM_EPS = 1e-6
SEG = 512
N_SEG = 9
TAIL_W = 256

NEG_BIG = -1e30
INT_MIN = -2 ** 31
INT_MAX = 2 ** 31 - 1
ZERO_TOP = 0x007FFFFF
LOG2E = 1.4426950408889634

VMEM_LIMIT = 56 * 1024 * 1024
PROJ_ROWS = 512
Q_TILE = 256
BIG = 1024
HEAD_GROUPS = ((0, 1, 2, 3),)
SUB_ROWS = 64
SELECT_TILE = 512
QROWS = 16
ATTN_PAGES = 16
IDX_PAGES = 32
TOKEN_HEAD_OUTS = (1, 3, 7, 9)


def _alibi_slopes():
    n = DIFF_HEADS + DSA_HEADS
    s = [2.0 ** (-8.0 * i / n) for i in range(1, n + 1)]
    return s[0::2], s[1::2]


SLOPES_DIFF, SLOPES_DSA = _alibi_slopes()
LAM_INIT = 0.8 - 0.6 * math.exp(-0.3 * 0)


def _nt_dot(a, b):
    return lax.dot_general(a, b, (((1,), (1,)), ((), ())), preferred_element_type=F32)


def _rms(x, g):
    ms = jnp.mean(x * x, axis=-1, keepdims=True)
    return x * lax.rsqrt(ms + NORM_EPS) * g


def _float_key(x):
    b = lax.bitcast_convert_type(x, I32)
    return b ^ (lax.shift_right_arithmetic(b, 31) & INT_MAX)


def _score_key(x, pos):
    return jnp.where(x == 0.0, ZERO_TOP - pos, _float_key(x))


def _store_token_head_rows(ref, z):
    heads = z.shape[1] // HEAD_W
    for h in range(heads):
        ref[pl.ds(h, z.shape[0], stride=heads), :] = z[:, h * HEAD_W:(h + 1) * HEAD_W]


def _proj_kernel(x_ref, g_ref, w_ref, wt_ref,
                 dq_ref, dk32_ref, dk16_ref, dv32_ref, dv16_ref, dg_ref,
                 sq_ref, sk32_ref, sk16_ref, sv32_ref, sv16_ref, sg_ref,
                 iq_ref, ik32_ref, ikk_ref, iw_ref):
    xn = _rms(x_ref[...], g_ref[...]).astype(BF16)

    def seg(i):
        return jnp.dot(xn, w_ref[:, i * SEG:(i + 1) * SEG], preferred_element_type=F32)

    dq_ref[...] = (seg(0) * (DIFF_QK ** -0.5 * LOG2E)).astype(BF16)
    z = seg(1); _store_token_head_rows(dk32_ref, z); dk16_ref[...] = z.astype(BF16)
    z = seg(2); _store_token_head_rows(dv32_ref, z); dv16_ref[...] = z.astype(BF16)
    dg_ref[...] = seg(3)
    sq_ref[...] = (seg(4) * (DSA_DIM ** -0.5 * LOG2E)).astype(BF16)
    z = seg(5); _store_token_head_rows(sk32_ref, z); sk16_ref[...] = z.astype(BF16)
    z = seg(6); _store_token_head_rows(sv32_ref, z); sv16_ref[...] = z.astype(BF16)
    sg_ref[...] = seg(7)
    iq_ref[...] = (seg(8) * IDX_DIM ** -0.5).astype(BF16)
    zt = jnp.dot(xn, wt_ref[...], preferred_element_type=F32)
    ik32_ref[...] = zt[:, :IDX_DIM]
    ikk_ref[...] = zt[:, :2 * IDX_DIM].astype(BF16)
    iw_ref[...] = zt[:, 2 * IDX_DIM:2 * IDX_DIM + IDX_HEADS] * IDX_HEADS ** -0.5


def _project(x, g, w_main, w_tail, tm):
    m, d = x.shape
    row = lambda w, dt: jax.ShapeDtypeStruct((m, w), dt)
    rspec = lambda w: pl.BlockSpec((tm, w), lambda i: (i, 0))
    heads = WIDTH // HEAD_W
    th_spec = pl.BlockSpec((tm * heads, HEAD_W), lambda i: (i, 0))
    th_shape = jax.ShapeDtypeStruct((m * heads, HEAD_W), F32)
    outs = [(WIDTH, BF16),
            (WIDTH, F32), (WIDTH, BF16), (WIDTH, F32), (WIDTH, BF16), (WIDTH, F32),
            (WIDTH, BF16),
            (WIDTH, F32), (WIDTH, BF16), (WIDTH, F32), (WIDTH, BF16), (WIDTH, F32),
            (WIDTH, BF16),
            (IDX_DIM, F32), (2 * IDX_DIM, BF16), (IDX_HEADS, F32)]
    return pl.pallas_call(
        _proj_kernel,
        grid=(m // tm,),
        in_specs=[rspec(d),
                  pl.BlockSpec((1, d), lambda i: (0, 0)),
                  pl.BlockSpec(w_main.shape, lambda i: (0, 0)),
                  pl.BlockSpec(w_tail.shape, lambda i: (0, 0))],
        out_specs=[th_spec if i in TOKEN_HEAD_OUTS else rspec(w) for i, (w, _) in enumerate(outs)],
        out_shape=[th_shape if i in TOKEN_HEAD_OUTS else row(w, dt) for i, (w, dt) in enumerate(outs)],
        compiler_params=pltpu.CompilerParams(dimension_semantics=("arbitrary",),
                                             vmem_limit_bytes=VMEM_LIMIT),
        name="in_proj",
    )(x, g, w_main, w_tail)


def _diff_lambda(lp):
    a = jnp.exp(jnp.sum(lp[0:1] * lp[1:2], axis=-1, keepdims=True))
    b = jnp.exp(jnp.sum(lp[2:3] * lp[3:4], axis=-1, keepdims=True))
    return a - b + LAM_INIT


def _diff_finish(o2, lam, subln, t):
    od = o2[:t] - lam * o2[t:]
    return _rms(od, subln) * (1.0 - LAM_INIT)


def _diff_prompt_kernel(lam_ref, subln_ref, q_ref, k_ref, v_ref, o_ref, m_scr, l_scr, acc_scr, *, t, big):
    qi = pl.program_id(1)
    lam = _diff_lambda(lam_ref[...])
    r = big // t
    n_big = qi // r
    row = lax.broadcasted_iota(I32, (2 * t, HEAD_W), 0)
    row = jnp.where(row >= t, row - t, row)
    rowf = row.astype(F32)
    colf = lax.broadcasted_iota(I32, (1, big), 1).astype(F32)
    lane = lax.broadcasted_iota(I32, (t, HEAD_W), 1)
    rel = lax.broadcasted_iota(I32, (2 * t, big), 1) - jnp.tile(row, (1, big // HEAD_W))
    visible = rel <= qi * t - n_big * big
    n = big // HEAD_W

    m_scr[...] = jnp.full(m_scr.shape, -jnp.inf, F32)
    l_scr[...] = jnp.zeros(l_scr.shape, F32)
    acc_scr[...] = jnp.zeros(acc_scr.shape, F32)
    qqs = []
    for h in range(DIFF_HEADS):
        qh = q_ref[:, h * HEAD_W:(h + 1) * HEAD_W].astype(F32)
        qqs.append(jnp.concatenate([jnp.where(lane < DIFF_QK, qh, 0.0), jnp.where(lane >= DIFF_QK, qh, 0.0)],
                                   axis=0).astype(BF16))

    def tile(jb, mask):
        kb = k_ref[pl.ds(jb * r, r)].reshape(big, WIDTH)
        vb = v_ref[pl.ds(jb * r, r)].reshape(big, WIDTH)
        dist = (qi * t - jb * big).astype(F32)
        for grp in HEAD_GROUPS:
            us = {h: _nt_dot(qqs[h], kb[:, h * HEAD_W:(h + 1) * HEAD_W]) for h in grp}
            ps = {}
            for h in grp:
                slope = SLOPES_DIFF[h] * LOG2E
                u = us[h] + slope * colf
                if mask is not None:
                    u = jnp.where(mask, u, NEG_BIG)
                shift = slope * rowf + slope * dist
                part = u[:, :HEAD_W]
                for c in range(1, n):
                    part = jnp.maximum(part, u[:, c * HEAD_W:(c + 1) * HEAD_W])
                m_prev = m_scr[h]
                m_new = jnp.maximum(m_prev, jnp.max(part, axis=1, keepdims=True) - shift)
                p = jnp.exp2(u - jnp.tile(m_new + shift, (1, n)))
                alpha = jnp.exp2(m_prev - m_new)
                psum = p[:, :HEAD_W]
                for c in range(1, n):
                    psum = psum + p[:, c * HEAD_W:(c + 1) * HEAD_W]
                l_scr[h] = alpha * l_scr[h] + psum
                m_scr[h] = m_new
                ps[h] = (p.astype(BF16), alpha)
            for h in grp:
                p, alpha = ps[h]
                acc_scr[h] = alpha * acc_scr[h] + jnp.dot(p, vb[:, h * HEAD_W:(h + 1) * HEAD_W],
                                                          preferred_element_type=F32)

    def body(jb, c):
        tile(jb, None)
        return c

    lax.fori_loop(0, n_big, body, 0)
    tile(n_big, visible)
    for h in range(DIFF_HEADS):
        o2 = acc_scr[h] / jnp.sum(l_scr[h], axis=1, keepdims=True)
        o_ref[:, h * HEAD_W:(h + 1) * HEAD_W] = _diff_finish(o2, lam, subln_ref[...], t)


def _diff_prompt(lam_p, subln, dq, dk, dv, batch, seq, t):
    nt = seq // t
    k3 = dk.reshape(batch * nt, t, WIDTH)
    v3 = dv.reshape(batch * nt, t, WIDTH)
    kv_spec = pl.BlockSpec((nt, t, WIDTH), lambda b, i: (b, 0, 0), pipeline_mode=pl.Buffered(1))
    return pl.pallas_call(
        functools.partial(_diff_prompt_kernel, t=t, big=BIG),
        grid=(batch, nt),
        in_specs=[pl.BlockSpec(lam_p.shape, lambda b, i: (0, 0)),
                  pl.BlockSpec(subln.shape, lambda b, i: (0, 0)),
                  pl.BlockSpec((t, WIDTH), lambda b, i: (b * nt + i, 0)),
                  kv_spec, kv_spec],
        out_specs=pl.BlockSpec((t, WIDTH), lambda b, i: (b * nt + i, 0)),
        out_shape=jax.ShapeDtypeStruct((batch * seq, WIDTH), F32),
        scratch_shapes=[pltpu.VMEM((DIFF_HEADS, 2 * t, HEAD_W), F32)] * 3,
        compiler_params=pltpu.CompilerParams(dimension_semantics=("arbitrary", "arbitrary"),
                                             vmem_limit_bytes=VMEM_LIMIT),
        name="diff_prompt",
    )(lam_p, subln, dq, k3, v3)


def _count_keys(key_ref, n_big, r, rows, strip, pred, operands):
    tk = key_ref.shape[2]
    lane = lax.broadcasted_iota(I32, (strip, 128), 1)
    parts = []
    for s in range(rows // strip):
        ops = [o[s * strip:(s + 1) * strip] for o in operands]

        def body(jb, cnt, s=s, ops=ops):
            for i in range(r):
                for c in range(tk // 128):
                    kt = key_ref[jb * r + i, s * strip:(s + 1) * strip, c * 128:(c + 1) * 128]
                    hit = pred(kt, lane + ((jb * r + i) * tk + c * 128), *ops)
                    cnt = cnt + jnp.where(hit, 1.0, 0.0)
            return cnt

        parts.append(lax.fori_loop(0, n_big, body, jnp.zeros((strip, 128), F32)))
    return jnp.sum(jnp.concatenate(parts, axis=0), axis=1, keepdims=True)


def _select_rows(key_ref, extra_ref, n_big, r, rows, strip, ksel, idx_bits):
    tk = key_ref.shape[2]
    extra_pos = key_ref.shape[0] * tk

    def count(pred, *operands):
        tot = _count_keys(key_ref, n_big, r, rows, strip, pred, operands)
        if extra_ref is not None:
            hit = pred(extra_ref[:, :1], extra_pos, *[o[:, :1] for o in operands])
            tot = tot + jnp.where(hit, 1.0, 0.0)
        return tot

    def unsettled(n_at):
        return jnp.max(jnp.where(n_at != ksel, 1, 0))

    def value_pass(state):
        i, thr, n_at, _ = state
        cand = thr + lax.shift_left(jnp.int32(1), 31 - i)
        n_ge = count(lambda kt, pos, c: kt >= c, cand)
        ok = n_ge >= ksel
        n_at = jnp.where(ok, n_ge, n_at)
        return i + 1, jnp.where(ok, cand, thr), n_at, unsettled(n_at)

    start = (jnp.int32(0), jnp.full((rows, 128), INT_MIN, I32), jnp.full((rows, 1), 2.0 * extra_pos + 2.0, F32),
             jnp.int32(1))
    _, thr, n_at, _ = lax.while_loop(lambda st: (st[0] < 32) & (st[3] > 0), value_pass, start)

    @pl.when(jnp.max(n_at) > ksel)
    def _():
        need = ksel - count(lambda kt, pos, th: kt > th, thr)

        def pos_pass(i, p):
            cand = p + lax.shift_left(jnp.int32(1), idx_bits - 1 - i)
            n_lt = count(lambda kt, pos, th, c: (kt == th) & (pos < c), thr, cand)
            return jnp.where(n_lt < need, cand, p)

        last = lax.fori_loop(0, idx_bits, pos_pass, jnp.zeros((rows, 128), I32))
        lane = lax.broadcasted_iota(I32, (rows, 128), 1)

        def demote(j, c):
            for ch in range(tk // 128):
                kt = key_ref[j, :, ch * 128:(ch + 1) * 128]
                drop = (kt == thr) & (lane + (j * tk + ch * 128) > last)
                key_ref[j, :, ch * 128:(ch + 1) * 128] = jnp.where(drop, INT_MIN, kt)
            return c

        lax.fori_loop(0, n_big * r, demote, 0)
        if extra_ref is not None:
            e = extra_ref[...]
            extra_ref[...] = jnp.where((e == thr) & (extra_pos > last), INT_MIN, e)

    return thr


def _dsa_prompt_kernel(q_ref, k_ref, v_ref, iq_ref, ikk_ref, iw_ref, o_ref,
                       key_scr, m_scr, l_scr, acc_scr, *, t, big, sub, ksel, idx_bits):
    qi = pl.program_id(1)
    r = big // t
    n_big = qi // r + 1
    row = lax.broadcasted_iota(I32, (t, t), 0)
    col = lax.broadcasted_iota(I32, (t, t), 1)
    causal = col <= row
    lane = lax.broadcasted_iota(I32, (t, HEAD_W), 1)

    iqs, ws = [], []
    for h in range(IDX_HEADS):
        pair = iq_ref[:, (h // 2) * HEAD_W:(h // 2 + 1) * HEAD_W].astype(F32)
        keep = (lane < IDX_DIM) if h % 2 == 0 else (lane >= IDX_DIM)
        iqs.append(jnp.where(keep, pair, 0.0).astype(BF16))
        ws.append(jnp.broadcast_to(iw_ref[:, h:h + 1], (t, t)))

    def index_tile(j, masked):
        ikt = ikk_ref[j]
        acc = jnp.zeros((t, t), F32)
        for h in range(IDX_HEADS):
            acc = acc + ws[h] * jnp.maximum(_nt_dot(iqs[h], ikt), 0.0)
        if masked:
            acc = jnp.where(causal, acc, -jnp.inf)
        key_scr[j] = _score_key(acc, col + j * t)

    def index_body(j, c):
        index_tile(j, False)
        return c

    lax.fori_loop(0, qi, index_body, 0)
    index_tile(qi, True)

    def pad_body(j, c):
        key_scr[j] = jnp.full((t, t), INT_MIN, I32)
        return c

    lax.fori_loop(qi + 1, n_big * r, pad_body, 0)

    thr_rows = _select_rows(key_scr, None, n_big, r, t, sub, ksel, idx_bits)

    n = big // HEAD_W
    thr = jnp.tile(thr_rows, (1, n))
    rowf = lax.broadcasted_iota(I32, (t, HEAD_W), 0).astype(F32)
    colf = lax.broadcasted_iota(I32, (1, big), 1).astype(F32)
    rel = lax.broadcasted_iota(I32, (t, big), 1) - lax.broadcasted_iota(I32, (t, big), 0)
    visible = rel <= qi * t - (n_big - 1) * big
    m_scr[...] = jnp.full(m_scr.shape, -jnp.inf, F32)
    l_scr[...] = jnp.zeros(l_scr.shape, F32)
    acc_scr[...] = jnp.zeros(acc_scr.shape, F32)

    def attend_tile(jb, mask):
        keys = jnp.concatenate([key_scr[jb * r + i] for i in range(r)], axis=1)
        sel = keys >= thr
        if mask is not None:
            sel = sel & mask
        selbias = jnp.where(sel, 0.0, NEG_BIG)
        kb = k_ref[pl.ds(jb * r, r)].reshape(big, WIDTH)
        vb = v_ref[pl.ds(jb * r, r)].reshape(big, WIDTH)
        dist = (qi * t - jb * big).astype(F32)
        for grp in HEAD_GROUPS:
            us = {h: _nt_dot(q_ref[:, h * HEAD_W:(h + 1) * HEAD_W], kb[:, h * HEAD_W:(h + 1) * HEAD_W]) for h in grp}
            ps = {}
            for h in grp:
                slope = SLOPES_DSA[h] * LOG2E
                u = us[h] + slope * colf + selbias
                shift = slope * rowf + slope * dist
                part = u[:, :HEAD_W]
                for c in range(1, n):
                    part = jnp.maximum(part, u[:, c * HEAD_W:(c + 1) * HEAD_W])
                m_prev = m_scr[h]
                m_new = jnp.maximum(m_prev, jnp.max(part, axis=1, keepdims=True) - shift)
                p = jnp.exp2(u - jnp.tile(m_new + shift, (1, n)))
                alpha = jnp.exp2(m_prev - m_new)
                psum = p[:, :HEAD_W]
                for c in range(1, n):
                    psum = psum + p[:, c * HEAD_W:(c + 1) * HEAD_W]
                l_scr[h] = alpha * l_scr[h] + psum
                m_scr[h] = m_new
                ps[h] = (p.astype(BF16), alpha)
            for h in grp:
                p, alpha = ps[h]
                acc_scr[h] = alpha * acc_scr[h] + jnp.dot(p, vb[:, h * HEAD_W:(h + 1) * HEAD_W],
                                                          preferred_element_type=F32)

    def attend_body(jb, c):
        attend_tile(jb, None)
        return c

    lax.fori_loop(0, n_big - 1, attend_body, 0)
    attend_tile(n_big - 1, visible)
    for h in range(DSA_HEADS):
        o_ref[:, h * HEAD_W:(h + 1) * HEAD_W] = acc_scr[h] / jnp.sum(l_scr[h], axis=1, keepdims=True)


def _dsa_prompt(sq, sk, sv, iq, ikk, iw, batch, seq, t):
    nt = seq // t
    ksel = min(TOPK_MAX, seq // 4)
    assert t >= ksel and t % 128 == 0
    idx_bits = max(1, int(seq).bit_length())
    whole = lambda w: pl.BlockSpec((nt, t, w), lambda b, i: (b, 0, 0), pipeline_mode=pl.Buffered(1))
    rows = lambda w: pl.BlockSpec((t, w), lambda b, i: (b * nt + i, 0))
    return pl.pallas_call(
        functools.partial(_dsa_prompt_kernel, t=t, big=BIG, sub=SUB_ROWS, ksel=ksel, idx_bits=idx_bits),
        grid=(batch, nt),
        in_specs=[rows(WIDTH), whole(WIDTH), whole(WIDTH), rows(WIDTH), whole(2 * IDX_DIM), rows(IDX_HEADS)],
        out_specs=rows(WIDTH),
        out_shape=jax.ShapeDtypeStruct((batch * seq, WIDTH), F32),
        scratch_shapes=[pltpu.VMEM((nt, t, t), I32),
                        pltpu.VMEM((DSA_HEADS, t, HEAD_W), F32), pltpu.VMEM((DSA_HEADS, t, HEAD_W), F32),
                        pltpu.VMEM((DSA_HEADS, t, HEAD_W), F32)],
        compiler_params=pltpu.CompilerParams(dimension_semantics=("arbitrary", "arbitrary"),
                                             vmem_limit_bytes=VMEM_LIMIT),
        name="dsa_prompt",
    )(sq, sk.reshape(batch * nt, t, WIDTH), sv.reshape(batch * nt, t, WIDTH), iq,
      ikk.reshape(batch * nt, t, 2 * IDX_DIM), iw)


def _idx_sample_kernel(pt_ref, iq_ref, iw_ref, *refs, g):
    page_refs, o_ref = refs[:g], refs[g]
    pad = QROWS - IDX_HEADS
    iq = jnp.concatenate([iq_ref[0].astype(F32), jnp.zeros((pad, IDX_DIM), F32)], axis=0).astype(BF16)
    w = jnp.concatenate([iw_ref[0], jnp.zeros((pad, 1), F32)], axis=0)
    for i in range(g):
        keys = page_refs[i][0, 0].astype(BF16)
        d = jnp.dot(iq, keys, preferred_element_type=F32)
        page = keys.shape[1]
        o_ref[0, :, i * page:(i + 1) * page] = jnp.sum(w * jnp.maximum(d, 0.0), axis=0, keepdims=True)


def _idx_sample(page_table, iq, iw, cache_ik, g):
    nb, n_pages = page_table.shape
    page = cache_ik.shape[2]
    cache_t = jnp.swapaxes(cache_ik, 2, 3)

    def page_spec(i):
        return pl.BlockSpec((1, 1, IDX_DIM, page), lambda b, j, pt: (0, pt[b, j * g + i], 0, 0))

    return pl.pallas_call(
        functools.partial(_idx_sample_kernel, g=g),
        grid_spec=pltpu.PrefetchScalarGridSpec(
            num_scalar_prefetch=1,
            grid=(nb, n_pages // g),
            in_specs=[pl.BlockSpec((1, IDX_HEADS, IDX_DIM), lambda b, j, pt: (b, 0, 0)),
                      pl.BlockSpec((1, IDX_HEADS, 1), lambda b, j, pt: (b, 0, 0))]
                     + [page_spec(i) for i in range(g)],
            out_specs=pl.BlockSpec((1, 1, g * page), lambda b, j, pt: (b, 0, j)),
        ),
        out_shape=jax.ShapeDtypeStruct((nb, 1, n_pages * page), F32),
        compiler_params=pltpu.CompilerParams(dimension_semantics=("arbitrary", "arbitrary"),
                                             vmem_limit_bytes=VMEM_LIMIT),
        name="idx_sample",
    )(page_table, iq.reshape(nb, IDX_HEADS, IDX_DIM), iw.reshape(nb, IDX_HEADS, 1), *([cache_t] * g))


def _select_sample_kernel(isc_ref, iq_ref, ikk_ref, iw_ref, bias_ref, new_ref, key_scr, extra_scr,
                          *, tk, sub, ksel, idx_bits):
    rows, past = isc_ref.shape
    nt = past // tk
    lane_t = lax.broadcasted_iota(I32, (rows, tk), 1)
    for j in range(nt):
        key_scr[j] = _score_key(isc_ref[:, j * tk:(j + 1) * tk], lane_t + j * tk)
    lane = lax.broadcasted_iota(I32, (rows, HEAD_W), 1)
    kk = ikk_ref[...].astype(F32)
    new = jnp.zeros((rows, 1), F32)
    for h in range(IDX_HEADS):
        pair = iq_ref[:, (h // 2) * HEAD_W:(h // 2 + 1) * HEAD_W].astype(F32)
        keep = (lane < IDX_DIM) if h % 2 == 0 else (lane >= IDX_DIM)
        d = jnp.sum(jnp.where(keep, pair * kk, 0.0), axis=1, keepdims=True)
        new = new + iw_ref[:, h:h + 1] * jnp.maximum(d, 0.0)
    extra_scr[...] = jnp.broadcast_to(_score_key(new, past), extra_scr.shape)

    thr_rows = _select_rows(key_scr, extra_scr, nt, 1, rows, sub, ksel, idx_bits)
    thr = jnp.tile(thr_rows, (1, tk // 128))
    for j in range(nt):
        bias_ref[:, j * tk:(j + 1) * tk] = jnp.where(key_scr[j] >= thr, 0.0, NEG_BIG)
    new_ref[...] = jnp.where(extra_scr[...] >= thr_rows, 0.0, NEG_BIG)


def _select_sample(isc, iq, ikk, iw, tk):
    rows, past = isc.shape
    ksel = min(TOPK_MAX, (past + 1) // 4)
    idx_bits = int(past + 1).bit_length()
    full = lambda a: pl.BlockSpec(a.shape, lambda i: (0,) * a.ndim)
    return pl.pallas_call(
        functools.partial(_select_sample_kernel, tk=tk, sub=min(SUB_ROWS, rows), ksel=ksel, idx_bits=idx_bits),
        grid=(1,),
        in_specs=[full(isc), full(iq), full(ikk), full(iw)],
        out_specs=[pl.BlockSpec((rows, past), lambda i: (0, 0)), pl.BlockSpec((rows, 128), lambda i: (0, 0))],
        out_shape=[jax.ShapeDtypeStruct((rows, past), F32), jax.ShapeDtypeStruct((rows, 128), F32)],
        scratch_shapes=[pltpu.VMEM((past // tk, rows, tk), I32), pltpu.VMEM((rows, 128), I32)],
        compiler_params=pltpu.CompilerParams(dimension_semantics=("arbitrary",), vmem_limit_bytes=VMEM_LIMIT),
        name="select_sample",
    )(isc, iq, ikk, iw)


def _attn_sample_kernel(pt_ref, lam_ref, subln_ref, dq_ref, dkn_ref, dvn_ref, sq_ref, skn_ref, svn_ref,
                        bias_ref, new_ref, *refs, g, page):
    dk_refs, dv_refs, sk_refs, sv_refs = refs[:g], refs[g:2 * g], refs[2 * g:3 * g], refs[3 * g:4 * g]
    od_ref, os_ref, md_scr, ld_scr, accd_scr, ms_scr, ls_scr, accs_scr = refs[4 * g:]
    j = pl.program_id(1)
    nj = pl.num_programs(1)
    gp = g * page
    past = nj * gp
    heads = DIFF_HEADS
    row = lax.broadcasted_iota(I32, (QROWS, HEAD_W), 0)
    lane = lax.broadcasted_iota(I32, (QROWS, HEAD_W), 1)
    keep_d = ((row == 0) & (lane < DIFF_QK)) | ((row == 1) & (lane >= DIFF_QK))
    keep_s = row == 0
    kpos = lax.broadcasted_iota(I32, (QROWS, gp), 1) + j * gp
    dist = (past - kpos).astype(F32)

    def head_q(q_ref, h, keep):
        return jnp.where(keep, q_ref[0][:, h * HEAD_W:(h + 1) * HEAD_W].astype(F32), 0.0).astype(BF16)

    def head_rows(refs_, h):
        return jnp.concatenate([r[0, 0, pl.ds(h, page, stride=heads), :] for r in refs_], axis=0).astype(BF16)

    qds = [head_q(dq_ref, h, keep_d) for h in range(heads)]
    qss = [head_q(sq_ref, h, keep_s) for h in range(heads)]

    @pl.when(j == 0)
    def _():
        for h in range(heads):
            hs = slice(h * HEAD_W, (h + 1) * HEAD_W)
            sd = jnp.sum(qds[h].astype(F32) * dkn_ref[0][:, hs].astype(F32), axis=1, keepdims=True)
            md_scr[h] = jnp.broadcast_to(sd, (QROWS, HEAD_W))
            ld_scr[h] = jnp.ones((QROWS, HEAD_W), F32)
            accd_scr[h] = jnp.broadcast_to(dvn_ref[0][:, hs].astype(F32), (QROWS, HEAD_W))
            ss = jnp.sum(qss[h].astype(F32) * skn_ref[0][:, hs].astype(F32), axis=1, keepdims=True)
            ms_scr[h] = jnp.broadcast_to(ss + new_ref[0][:, :1], (QROWS, HEAD_W))
            ls_scr[h] = jnp.ones((QROWS, HEAD_W), F32)
            accs_scr[h] = jnp.broadcast_to(svn_ref[0][:, hs].astype(F32), (QROWS, HEAD_W))

    selbias = bias_ref[0]
    jobs = []
    for h in range(heads):
        jobs.append((h, qds[h], dk_refs, dv_refs, -(SLOPES_DIFF[h] * LOG2E) * dist, md_scr, ld_scr, accd_scr))
        jobs.append((h, qss[h], sk_refs, sv_refs, -(SLOPES_DSA[h] * LOG2E) * dist + selbias, ms_scr, ls_scr, accs_scr))
    scores = [_nt_dot(q, head_rows(k_refs, h)) + bias for h, q, k_refs, _, bias, _, _, _ in jobs]
    probs = []
    for (h, _, _, _, _, m_scr, l_scr, acc_scr), s in zip(jobs, scores):
        m_prev = m_scr[h]
        m_new = jnp.maximum(m_prev, jnp.max(s, axis=1, keepdims=True))
        p = jnp.exp2(s - m_new[:, :1])
        alpha = jnp.exp2(m_prev - m_new)
        l_scr[h] = alpha * l_scr[h] + jnp.sum(p, axis=1, keepdims=True)
        m_scr[h] = m_new
        probs.append((p.astype(BF16), alpha))
    for (h, _, _, v_refs, _, _, _, acc_scr), (p, alpha) in zip(jobs, probs):
        acc_scr[h] = alpha * acc_scr[h] + jnp.dot(p, head_rows(v_refs, h), preferred_element_type=F32)

    @pl.when(j == nj - 1)
    def _():
        lam = _diff_lambda(lam_ref[...])
        for h in range(heads):
            hs = slice(h * HEAD_W, (h + 1) * HEAD_W)
            od = accd_scr[h] / ld_scr[h]
            od_ref[0, :, hs] = _rms(od[0:1] - lam * od[1:2], subln_ref[...]) * (1.0 - LAM_INIT)
            os_ref[0, :, hs] = (accs_scr[h] / ls_scr[h])[0:1]


def _attn_sample(page_table, lam_p, subln, dq, dk_new, dv_new, sq, sk_new, sv_new, selbias, selnew,
                 cache_dk, cache_dv, cache_sk, cache_sv, g):
    nb, n_pages = page_table.shape
    n_pool, page, heads = cache_dk.shape[1:4]
    assert heads == DIFF_HEADS == DSA_HEADS
    gp = g * page
    flat = lambda c: c.reshape(c.shape[0], n_pool, page * heads, HEAD_W)
    vec = lambda a: a.reshape(nb, 1, a.shape[-1])

    def page_spec(i):
        return pl.BlockSpec((1, 1, page * heads, HEAD_W), lambda b, j, pt: (0, pt[b, j * g + i], 0, 0))

    per_b = lambda w: pl.BlockSpec((1, 1, w), lambda b, j, pt: (b, 0, 0))
    const = lambda a: pl.BlockSpec(a.shape, lambda b, j, pt: (0,) * a.ndim)
    pages = [page_spec(i) for i in range(g)]
    state = [pltpu.VMEM((heads, QROWS, HEAD_W), F32)] * 3
    return pl.pallas_call(
        functools.partial(_attn_sample_kernel, g=g, page=page),
        grid_spec=pltpu.PrefetchScalarGridSpec(
            num_scalar_prefetch=1,
            grid=(nb, n_pages // g),
            in_specs=[const(lam_p), const(subln)] + [per_b(WIDTH)] * 6
                     + [pl.BlockSpec((1, 1, gp), lambda b, j, pt: (b, 0, j)), per_b(128)] + pages * 4,
            out_specs=[per_b(WIDTH), per_b(WIDTH)],
            scratch_shapes=state * 2,
        ),
        out_shape=[jax.ShapeDtypeStruct((nb, 1, WIDTH), F32)] * 2,
        compiler_params=pltpu.CompilerParams(dimension_semantics=("arbitrary", "arbitrary"),
                                             vmem_limit_bytes=VMEM_LIMIT),
        name="attn_sample",
    )(page_table, lam_p, subln, vec(dq), vec(dk_new), vec(dv_new), vec(sq), vec(sk_new), vec(sv_new),
      selbias.reshape(nb, 1, -1), vec(selnew),
      *([flat(cache_dk)] * g), *([flat(cache_dv)] * g), *([flat(cache_sk)] * g), *([flat(cache_sv)] * g))


def _silu(x):
    return x / (1.0 + jnp.exp(-x))


def _residual_kernel(h_ref, od_ref, os_ref, dg_ref, sg_ref, p_ref, wo_ref, gp_ref, wg_ref, wp_ref, gf_ref, y_ref):
    a = (od_ref[...] * _silu(dg_ref[...])).astype(BF16)
    b = (os_ref[...] * _silu(sg_ref[...])).astype(BF16)
    h = h_ref[...] + jnp.dot(a, wo_ref[:WIDTH, :], preferred_element_type=F32) \
        + jnp.dot(b, wo_ref[WIDTH:, :], preferred_element_type=F32)
    gate_in = _rms(h, gp_ref[...]).astype(BF16)
    gate = 1.0 / (1.0 + jnp.exp(-jnp.dot(gate_in, wg_ref[...], preferred_element_type=F32)))
    h = h + gate * jnp.dot(p_ref[...].astype(BF16), wp_ref[...], preferred_element_type=F32)
    y_ref[...] = _rms(h, gf_ref[...])


def _residual(h, od, osp, dg, sg, p, w_out, g_ple, w_gate, w_proj, g_final, tm):
    m, d = h.shape
    rows = lambda w: pl.BlockSpec((tm, w), lambda i: (i, 0))
    const = lambda a: pl.BlockSpec(a.shape, lambda i: (0,) * a.ndim)
    return pl.pallas_call(
        _residual_kernel,
        grid=(m // tm,),
        in_specs=[rows(d), rows(WIDTH), rows(WIDTH), rows(WIDTH), rows(WIDTH), rows(p.shape[1]),
                  const(w_out), const(g_ple), const(w_gate), const(w_proj), const(g_final)],
        out_specs=rows(d),
        out_shape=jax.ShapeDtypeStruct((m, d), F32),
        compiler_params=pltpu.CompilerParams(dimension_semantics=("arbitrary",), vmem_limit_bytes=VMEM_LIMIT),
        name="residual",
    )(h, od, osp, dg, sg, p, w_out, g_ple, w_gate, w_proj, g_final)


def _row_tile(m, want):
    t = min(m, want)
    assert m % t == 0
    return t


def kernel(x_prompt, x_sample, cache_diff_k, cache_diff_v, cache_dsa_k, cache_dsa_v, cache_idx_k, page_table,
           p_prompt, p_sample, norm_mix, w_in, diff_lambda, diff_subln, w_out, norm_ple, w_ple_gate, w_ple_proj,
           norm_final):
    batch, seq, d_model = x_prompt.shape
    nb = x_sample.shape[0]
    assert norm_mix.shape[0] == 1 and x_sample.shape[1] == 1
    n_main = N_SEG * SEG
    n_tail = IDX_DIM + IDX_HEADS
    assert w_in.shape[2] == n_main + n_tail

    w = w_in[0]
    w_main = w[:, :n_main].astype(BF16)
    ik_cols = w[:, n_main:n_main + IDX_DIM]
    w_tail = jnp.concatenate(
        [ik_cols, ik_cols, w[:, n_main + IDX_DIM:], jnp.zeros((d_model, TAIL_W - 2 * IDX_DIM - IDX_HEADS), F32)],
        axis=1).astype(BF16)
    g_mix = norm_mix[0][None]
    subln = diff_subln[0][None]
    lam_p = diff_lambda[0]
    res_w = (w_out[0].astype(BF16), norm_ple[0][None], w_ple_gate[0].astype(BF16), w_ple_proj[0].astype(BF16),
             norm_final[None])

    mp = batch * seq
    xp = x_prompt.reshape(mp, d_model)
    (dq, dk32, dk16, dv32, dv16, dg, sq, sk32, sk16, sv32, sv16, sg, iq, ik32, ikk, iw) = _project(
        xp, g_mix, w_main, w_tail, _row_tile(mp, PROJ_ROWS))
    t = _row_tile(seq, Q_TILE)
    assert seq % BIG == 0 and BIG % t == 0
    od = _diff_prompt(lam_p, subln, dq, dk16, dv16, batch, seq, t)
    osp = _dsa_prompt(sq, sk16, sv16, iq, ikk, iw, batch, seq, t)
    y_prompt = _residual(xp, od, osp, dg, sg, p_prompt[0].reshape(mp, -1), *res_w, _row_tile(mp, PROJ_ROWS))

    xs = x_sample.reshape(nb, d_model)
    (sdq, sdk32, sdk16, sdv32, sdv16, sdg, ssq, ssk32, ssk16, ssv32, ssv16, ssg, siq, sik32, sikk, siw) = _project(
        xs, g_mix, w_main, w_tail, _row_tile(nb, PROJ_ROWS))
    n_pages = page_table.shape[1]
    isc = _idx_sample(page_table, siq, siw, cache_idx_k, math.gcd(n_pages, IDX_PAGES)).reshape(nb, -1)
    selbias, selnew = _select_sample(isc, siq, sikk, siw, _row_tile(isc.shape[1], SELECT_TILE))
    od_s, os_s = _attn_sample(page_table, lam_p, subln, sdq, sdk16, sdv16, ssq, ssk16, ssv16, selbias, selnew,
                              cache_diff_k, cache_diff_v, cache_dsa_k, cache_dsa_v, math.gcd(n_pages, ATTN_PAGES))
    y_sample = _residual(xs, od_s.reshape(nb, WIDTH), os_s.reshape(nb, WIDTH), sdg, ssg,
                         p_sample[0].reshape(nb, -1), *res_w, _row_tile(nb, PROJ_ROWS))

    ph = lambda a, hd: a.reshape(1, batch, seq, hd, -1)
    sh = lambda a, hd: a.reshape(1, nb, 1, hd, -1)
    return (y_prompt.reshape(batch, seq, d_model), y_sample.reshape(nb, 1, d_model),
            ph(dk32, DIFF_HEADS), ph(dv32, DIFF_HEADS), ph(sk32, DSA_HEADS), ph(sv32, DSA_HEADS),
            ik32.reshape(1, batch, seq, IDX_DIM),
            sh(sdk32, DIFF_HEADS), sh(sdv32, DIFF_HEADS), sh(ssk32, DSA_HEADS), sh(ssv32, DSA_HEADS),
            sik32.reshape(1, nb, 1, IDX_DIM))
```

```python
import functools
import math

import jax
import jax.numpy as jnp
import numpy as np
from jax import lax
from jax.experimental import pallas as pl
from jax.experimental.pallas import tpu as pltpu

F32 = jnp.float32
BF16 = jnp.bfloat16
I32 = jnp.int32

DIFF_HEADS = 4
DIFF_QK = 64
DSA_HEADS = 4
DSA_DIM = 128
IDX_HEADS = 8
IDX_DIM = 64
HEAD_W = 128
WIDTH = 512
TOPK_MAX = 256
NORM_EPS = 1e-6
SEG = 512
N_SEG = 9
TAIL_W = 256

NEG_BIG = -1e30
INT_MIN = -2 ** 31
INT_MAX = 2 ** 31 - 1
ZERO_TOP = 0x007FFFFF
LOG2E = 1.4426950408889634

VMEM_LIMIT = 56 * 1024 * 1024
PROJ_ROWS = 512
Q_TILE = 256
BIG = 1024
HEAD_GROUPS = ((0, 1, 2, 3),)
SUB_ROWS = 64
SELECT_TILE = 512
QROWS = 16
ATTN_PAGES = 16
IDX_PAGES = 32
TOKEN_HEAD_OUTS = (1, 3, 7, 9)


def _alibi_slopes():
    n = DIFF_HEADS + DSA_HEADS
    s = [2.0 ** (-8.0 * i / n) for i in range(1, n + 1)]
    return s[0::2], s[1::2]


SLOPES_DIFF, SLOPES_DSA = _alibi_slopes()
LAM_INIT = 0.8 - 0.6 * math.exp(-0.3 * 0)


def _nt_dot(a, b):
    return lax.dot_general(a, b, (((1,), (1,)), ((), ())), preferred_element_type=F32)


def _rms(x, g):
    ms = jnp.mean(x * x, axis=-1, keepdims=True)
    return x * lax.rsqrt(ms + NORM_EPS) * g


def _float_key(x):
    b = lax.bitcast_convert_type(x, I32)
    return b ^ (lax.shift_right_arithmetic(b, 31) & INT_MAX)


def _score_key(x, pos):
    return jnp.where(x == 0.0, ZERO_TOP - pos, _float_key(x))


def _store_token_head_rows(ref, z):
    heads = z.shape[1] // HEAD_W
    for h in range(heads):
        ref[pl.ds(h, z.shape[0], stride=heads), :] = z[:, h * HEAD_W:(h + 1) * HEAD_W]


def _proj_kernel(x_ref, g_ref, w_ref, wt_ref,
                 dq_ref, dk32_ref, dk16_ref, dv32_ref, dv16_ref, dg_ref,
                 sq_ref, sk32_ref, sk16_ref, sv32_ref, sv16_ref, sg_ref,
                 iq_ref, ik32_ref, ikk_ref, iw_ref):
    xn = _rms(x_ref[...], g_ref[...]).astype(BF16)

    def seg(i):
        return jnp.dot(xn, w_ref[:, i * SEG:(i + 1) * SEG], preferred_element_type=F32)

    dq_ref[...] = (seg(0) * (DIFF_QK ** -0.5 * LOG2E)).astype(BF16)
    z = seg(1); _store_token_head_rows(dk32_ref, z); dk16_ref[...] = z.astype(BF16)
    z = seg(2); _store_token_head_rows(dv32_ref, z); dv16_ref[...] = z.astype(BF16)
    dg_ref[...] = seg(3)
    sq_ref[...] = (seg(4) * (DSA_DIM ** -0.5 * LOG2E)).astype(BF16)
    z = seg(5); _store_token_head_rows(sk32_ref, z); sk16_ref[...] = z.astype(BF16)
    z = seg(6); _store_token_head_rows(sv32_ref, z); sv16_ref[...] = z.astype(BF16)
    sg_ref[...] = seg(7)
    iq_ref[...] = (seg(8) * IDX_DIM ** -0.5).astype(BF16)
    zt = jnp.dot(xn, wt_ref[...], preferred_element_type=F32)
    ik32_ref[...] = zt[:, :IDX_DIM]
    ikk_ref[...] = zt[:, :2 * IDX_DIM].astype(BF16)
    iw_ref[...] = zt[:, 2 * IDX_DIM:2 * IDX_DIM + IDX_HEADS] * IDX_HEADS ** -0.5


def _project(x, g, w_main, w_tail, tm):
    m, d = x.shape
    row = lambda w, dt: jax.ShapeDtypeStruct((m, w), dt)
    rspec = lambda w: pl.BlockSpec((tm, w), lambda i: (i, 0))
    heads = WIDTH // HEAD_W
    th_spec = pl.BlockSpec((tm * heads, HEAD_W), lambda i: (i, 0))
    th_shape = jax.ShapeDtypeStruct((m * heads, HEAD_W), F32)
    outs = [(WIDTH, BF16),
            (WIDTH, F32), (WIDTH, BF16), (WIDTH, F32), (WIDTH, BF16), (WIDTH, F32),
            (WIDTH, BF16),
            (WIDTH, F32), (WIDTH, BF16), (WIDTH, F32), (WIDTH, BF16), (WIDTH, F32),
            (WIDTH, BF16),
            (IDX_DIM, F32), (2 * IDX_DIM, BF16), (IDX_HEADS, F32)]
    return pl.pallas_call(
        _proj_kernel,
        grid=(m // tm,),
        in_specs=[rspec(d),
                  pl.BlockSpec((1, d), lambda i: (0, 0)),
                  pl.BlockSpec(w_main.shape, lambda i: (0, 0)),
                  pl.BlockSpec(w_tail.shape, lambda i: (0, 0))],
        out_specs=[th_spec if i in TOKEN_HEAD_OUTS else rspec(w) for i, (w, _) in enumerate(outs)],
        out_shape=[th_shape if i in TOKEN_HEAD_OUTS else row(w, dt) for i, (w, dt) in enumerate(outs)],
        compiler_params=pltpu.CompilerParams(dimension_semantics=("arbitrary",),
                                             vmem_limit_bytes=VMEM_LIMIT),
        name="in_proj",
    )(x, g, w_main, w_tail)


def _diff_lambda(lp):
    a = jnp.exp(jnp.sum(lp[0:1] * lp[1:2], axis=-1, keepdims=True))
    b = jnp.exp(jnp.sum(lp[2:3] * lp[3:4], axis=-1, keepdims=True))
    return a - b + LAM_INIT


def _diff_finish(o2, lam, subln, t):
    od = o2[:t] - lam * o2[t:]
    return _rms(od, subln) * (1.0 - LAM_INIT)


def _diff_prompt_kernel(lam_ref, subln_ref, q_ref, k_ref, v_ref, o_ref, m_scr, l_scr, acc_scr, *, t, big):
    qi = pl.program_id(1)
    lam = _diff_lambda(lam_ref[...])
    r = big // t
    n_big = qi // r
    row = lax.broadcasted_iota(I32, (2 * t, HEAD_W), 0)
    row = jnp.where(row >= t, row - t, row)
    rowf = row.astype(F32)
    colf = lax.broadcasted_iota(I32, (1, big), 1).astype(F32)
    lane = lax.broadcasted_iota(I32, (t, HEAD_W), 1)
    half = big // 2
    first_tail = n_big * r
    limit = (qi - first_tail) * t
    rel = lax.broadcasted_iota(I32, (2 * t, half), 1) - jnp.tile(row, (1, half // HEAD_W))

    m_scr[...] = jnp.full(m_scr.shape, -jnp.inf, F32)
    l_scr[...] = jnp.zeros(l_scr.shape, F32)
    acc_scr[...] = jnp.zeros(acc_scr.shape, F32)
    qqs = []
    for h in range(DIFF_HEADS):
        qh = q_ref[:, h * HEAD_W:(h + 1) * HEAD_W].astype(F32)
        qqs.append(jnp.concatenate([jnp.where(lane < DIFF_QK, qh, 0.0), jnp.where(lane >= DIFF_QK, qh, 0.0)],
                                   axis=0).astype(BF16))

    def tile(first, width, mask):
        n = width // HEAD_W
        kb = k_ref[pl.ds(first, width // t)].reshape(width, WIDTH)
        vb = v_ref[pl.ds(first, width // t)].reshape(width, WIDTH)
        dist = ((qi - first) * t).astype(F32)
        colw = colf[:, :width]
        for grp in HEAD_GROUPS:
            us = {h: _nt_dot(qqs[h], kb[:, h * HEAD_W:(h + 1) * HEAD_W]) for h in grp}
            ps = {}
            for h in grp:
                slope = SLOPES_DIFF[h] * LOG2E
                u = us[h] + slope * colw
                if mask is not None:
                    u = jnp.where(mask, u, NEG_BIG)
                shift = slope * rowf + slope * dist
                part = u[:, :HEAD_W]
                for c in range(1, n):
                    part = jnp.maximum(part, u[:, c * HEAD_W:(c + 1) * HEAD_W])
                m_prev = m_scr[h]
                m_new = jnp.maximum(m_prev, jnp.max(part, axis=1, keepdims=True) - shift)
                p = jnp.exp2(u - jnp.tile(m_new + shift, (1, n)))
                alpha = jnp.exp2(m_prev - m_new)
                psum = p[:, :HEAD_W]
                for c in range(1, n):
                    psum = psum + p[:, c * HEAD_W:(c + 1) * HEAD_W]
                l_scr[h] = alpha * l_scr[h] + psum
                m_scr[h] = m_new
                ps[h] = (p.astype(BF16), alpha)
            for h in grp:
                p, alpha = ps[h]
                acc_scr[h] = alpha * acc_scr[h] + jnp.dot(p, vb[:, h * HEAD_W:(h + 1) * HEAD_W],
                                                          preferred_element_type=F32)

    def body(jb, c):
        tile(jb * r, big, None)
        return c

    lax.fori_loop(0, n_big, body, 0)
    tile(first_tail, half, rel <= limit)

    @pl.when(limit >= half)
    def _():
        tile(first_tail + r // 2, half, rel <= limit - half)

    for h in range(DIFF_HEADS):
        o2 = acc_scr[h] / jnp.sum(l_scr[h], axis=1, keepdims=True)
        o_ref[:, h * HEAD_W:(h + 1) * HEAD_W] = _diff_finish(o2, lam, subln_ref[...], t)


def _diff_prompt(lam_p, subln, dq, dk, dv, batch, seq, t):
    nt = seq // t
    k3 = dk.reshape(batch * nt, t, WIDTH)
    v3 = dv.reshape(batch * nt, t, WIDTH)
    kv_spec = pl.BlockSpec((nt, t, WIDTH), lambda b, i: (b, 0, 0), pipeline_mode=pl.Buffered(1))
    return pl.pallas_call(
        functools.partial(_diff_prompt_kernel, t=t, big=BIG),
        grid=(batch, nt),
        in_specs=[pl.BlockSpec(lam_p.shape, lambda b, i: (0, 0)),
                  pl.BlockSpec(subln.shape, lambda b, i: (0, 0)),
                  pl.BlockSpec((t, WIDTH), lambda b, i: (b * nt + i, 0)),
                  kv_spec, kv_spec],
        out_specs=pl.BlockSpec((t, WIDTH), lambda b, i: (b * nt + i, 0)),
        out_shape=jax.ShapeDtypeStruct((batch * seq, WIDTH), F32),
        scratch_shapes=[pltpu.VMEM((DIFF_HEADS, 2 * t, HEAD_W), F32)] * 3,
        compiler_params=pltpu.CompilerParams(dimension_semantics=("arbitrary", "arbitrary"),
                                             vmem_limit_bytes=VMEM_LIMIT),
        name="diff_prompt",
    )(lam_p, subln, dq, k3, v3)


def _count_keys(key_ref, n_big, r, rows, strip, pred, operands):
    tk = key_ref.shape[2]
    lane = lax.broadcasted_iota(I32, (strip, 128), 1)
    parts = []
    for s in range(rows // strip):
        ops = [o[s * strip:(s + 1) * strip] for o in operands]

        def body(jb, cnt, s=s, ops=ops):
            for i in range(r):
                for c in range(tk // 128):
                    kt = key_ref[jb * r + i, s * strip:(s + 1) * strip, c * 128:(c + 1) * 128]
                    hit = pred(kt, lane + ((jb * r + i) * tk + c * 128), *ops)
                    cnt = cnt + jnp.where(hit, 1.0, 0.0)
            return cnt

        parts.append(lax.fori_loop(0, n_big, body, jnp.zeros((strip, 128), F32)))
    return jnp.sum(jnp.concatenate(parts, axis=0), axis=1, keepdims=True)


def _select_rows(key_ref, extra_ref, n_big, r, rows, strip, ksel, idx_bits):
    tk = key_ref.shape[2]
    extra_pos = key_ref.shape[0] * tk

    def count(pred, *operands):
        tot = _count_keys(key_ref, n_big, r, rows, strip, pred, operands)
        if extra_ref is not None:
            hit = pred(extra_ref[:, :1], extra_pos, *[o[:, :1] for o in operands])
            tot = tot + jnp.where(hit, 1.0, 0.0)
        return tot

    def unsettled(n_at):
        return jnp.max(jnp.where(n_at != ksel, 1, 0))

    def fix_bit(i, thr, n_at):
        cand = thr + lax.shift_left(jnp.int32(1), 31 - i)
        n_ge = count(lambda kt, pos, c: kt >= c, cand)
        ok = n_ge >= ksel
        return jnp.where(ok, cand, thr), jnp.where(ok, n_ge, n_at)

    def value_pass(state):
        i, thr, n_at, _ = state
        thr, n_at = fix_bit(i, thr, n_at)
        thr, n_at = fix_bit(i + 1, thr, n_at)
        return i + 2, thr, n_at, unsettled(n_at)

    start = (jnp.int32(0), jnp.full((rows, 128), INT_MIN, I32), jnp.full((rows, 1), 2.0 * extra_pos + 2.0, F32),
             jnp.int32(1))
    _, thr, n_at, _ = lax.while_loop(lambda st: (st[0] < 32) & (st[3] > 0), value_pass, start)

    @pl.when(jnp.max(n_at) > ksel)
    def _():
        need = ksel - count(lambda kt, pos, th: kt > th, thr)

        def pos_pass(i, p):
            cand = p + lax.shift_left(jnp.int32(1), idx_bits - 1 - i)
            n_lt = count(lambda kt, pos, th, c: (kt == th) & (pos < c), thr, cand)
            return jnp.where(n_lt < need, cand, p)

        last = lax.fori_loop(0, idx_bits, pos_pass, jnp.zeros((rows, 128), I32))
        lane = lax.broadcasted_iota(I32, (rows, 128), 1)

        def demote(j, c):
            for ch in range(tk // 128):
                kt = key_ref[j, :, ch * 128:(ch + 1) * 128]
                drop = (kt == thr) & (lane + (j * tk + ch * 128) > last)
                key_ref[j, :, ch * 128:(ch + 1) * 128] = jnp.where(drop, INT_MIN, kt)
            return c

        lax.fori_loop(0, n_big * r, demote, 0)
        if extra_ref is not None:
            e = extra_ref[...]
            extra_ref[...] = jnp.where((e == thr) & (extra_pos > last), INT_MIN, e)

    return thr


def _dsa_prompt_kernel(q_ref, k_ref, v_ref, iq_ref, ikk_ref, iw_ref, o_ref,
                       key_scr, m_scr, l_scr, acc_scr, *, t, big, sub, ksel, idx_bits):
    qi = pl.program_id(1)
    r = big // t
    n_big = qi // r + 1
    row = lax.broadcasted_iota(I32, (t, t), 0)
    col = lax.broadcasted_iota(I32, (t, t), 1)
    causal = col <= row
    lane = lax.broadcasted_iota(I32, (t, HEAD_W), 1)

    iqs, ws = [], []
    for h in range(IDX_HEADS):
        pair = iq_ref[:, (h // 2) * HEAD_W:(h // 2 + 1) * HEAD_W].astype(F32)
        keep = (lane < IDX_DIM) if h % 2 == 0 else (lane >= IDX_DIM)
        iqs.append(jnp.where(keep, pair, 0.0).astype(BF16))
        ws.append(jnp.broadcast_to(iw_ref[:, h:h + 1], (t, t)))

    def index_tile(j, masked):
        ikt = ikk_ref[j]
        acc = jnp.zeros((t, t), F32)
        for h in range(IDX_HEADS):
            acc = acc + ws[h] * jnp.maximum(_nt_dot(iqs[h], ikt), 0.0)
        if masked:
            acc = jnp.where(causal, acc, -jnp.inf)
        key_scr[j] = _score_key(acc, col + j * t)

    def index_body(j, c):
        index_tile(j, False)
        return c

    lax.fori_loop(0, qi, index_body, 0)
    index_tile(qi, True)

    def pad_body(j, c):
        key_scr[j] = jnp.full((t, t), INT_MIN, I32)
        return c

    lax.fori_loop(qi + 1, n_big * r, pad_body, 0)

    thr_rows = _select_rows(key_scr, None, n_big, r, t, sub, ksel, idx_bits)

    thr = jnp.tile(thr_rows, (1, big // HEAD_W))
    rowf = lax.broadcasted_iota(I32, (t, HEAD_W), 0).astype(F32)
    colf = lax.broadcasted_iota(I32, (1, big), 1).astype(F32)
    half = big // 2
    first_tail = (n_big - 1) * r
    limit = (qi - first_tail) * t
    rel = lax.broadcasted_iota(I32, (t, half), 1) - lax.broadcasted_iota(I32, (t, half), 0)
    m_scr[...] = jnp.full(m_scr.shape, -jnp.inf, F32)
    l_scr[...] = jnp.zeros(l_scr.shape, F32)
    acc_scr[...] = jnp.zeros(acc_scr.shape, F32)

    def attend_tile(first, width, mask):
        n = width // HEAD_W
        keys = jnp.concatenate([key_scr[first + i] for i in range(width // t)], axis=1)
        sel = keys >= thr[:, :width]
        if mask is not None:
            sel = sel & mask
        selbias = jnp.where(sel, 0.0, NEG_BIG)
        kb = k_ref[pl.ds(first, width // t)].reshape(width, WIDTH)
        vb = v_ref[pl.ds(first, width // t)].reshape(width, WIDTH)
        dist = ((qi - first) * t).astype(F32)
        colw = colf[:, :width]
        for grp in HEAD_GROUPS:
            us = {h: _nt_dot(q_ref[:, h * HEAD_W:(h + 1) * HEAD_W], kb[:, h * HEAD_W:(h + 1) * HEAD_W]) for h in grp}
            ps = {}
            for h in grp:
                slope = SLOPES_DSA[h] * LOG2E
                u = us[h] + slope * colw + selbias
                shift = slope * rowf + slope * dist
                part = u[:, :HEAD_W]
                for c in range(1, n):
                    part = jnp.maximum(part, u[:, c * HEAD_W:(c + 1) * HEAD_W])
                m_prev = m_scr[h]
                m_new = jnp.maximum(m_prev, jnp.max(part, axis=1, keepdims=True) - shift)
                p = jnp.exp2(u - jnp.tile(m_new + shift, (1, n)))
                alpha = jnp.exp2(m_prev - m_new)
                psum = p[:, :HEAD_W]
                for c in range(1, n):
                    psum = psum + p[:, c * HEAD_W:(c + 1) * HEAD_W]
                l_scr[h] = alpha * l_scr[h] + psum
                m_scr[h] = m_new
                ps[h] = (p.astype(BF16), alpha)
            for h in grp:
                p, alpha = ps[h]
                acc_scr[h] = alpha * acc_scr[h] + jnp.dot(p, vb[:, h * HEAD_W:(h + 1) * HEAD_W],
                                                          preferred_element_type=F32)

    def attend_body(jb, c):
        attend_tile(jb * r, big, None)
        return c

    lax.fori_loop(0, n_big - 1, attend_body, 0)
    attend_tile(first_tail, half, rel <= limit)

    @pl.when(limit >= half)
    def _():
        attend_tile(first_tail + r // 2, half, rel <= limit - half)

    for h in range(DSA_HEADS):
        o_ref[:, h * HEAD_W:(h + 1) * HEAD_W] = acc_scr[h] / jnp.sum(l_scr[h], axis=1, keepdims=True)


def _dsa_prompt(sq, sk, sv, iq, ikk, iw, batch, seq, t):
    nt = seq // t
    ksel = min(TOPK_MAX, seq // 4)
    assert t >= ksel and t % 128 == 0
    idx_bits = max(1, int(seq).bit_length())
    whole = lambda w: pl.BlockSpec((nt, t, w), lambda b, i: (b, 0, 0), pipeline_mode=pl.Buffered(1))
    rows = lambda w: pl.BlockSpec((t, w), lambda b, i: (b * nt + i, 0))
    return pl.pallas_call(
        functools.partial(_dsa_prompt_kernel, t=t, big=BIG, sub=SUB_ROWS, ksel=ksel, idx_bits=idx_bits),
        grid=(batch, nt),
        in_specs=[rows(WIDTH), whole(WIDTH), whole(WIDTH), rows(WIDTH), whole(2 * IDX_DIM), rows(IDX_HEADS)],
        out_specs=rows(WIDTH),
        out_shape=jax.ShapeDtypeStruct((batch * seq, WIDTH), F32),
        scratch_shapes=[pltpu.VMEM((nt, t, t), I32),
                        pltpu.VMEM((DSA_HEADS, t, HEAD_W), F32), pltpu.VMEM((DSA_HEADS, t, HEAD_W), F32),
                        pltpu.VMEM((DSA_HEADS, t, HEAD_W), F32)],
        compiler_params=pltpu.CompilerParams(dimension_semantics=("arbitrary", "arbitrary"),
                                             vmem_limit_bytes=VMEM_LIMIT),
        name="dsa_prompt",
    )(sq, sk.reshape(batch * nt, t, WIDTH), sv.reshape(batch * nt, t, WIDTH), iq,
      ikk.reshape(batch * nt, t, 2 * IDX_DIM), iw)


def _idx_sample_kernel(pt_ref, iq_ref, iw_ref, *refs, g):
    page_refs, o_ref = refs[:g], refs[g]
    pad = QROWS - IDX_HEADS
    iq = jnp.concatenate([iq_ref[0].astype(F32), jnp.zeros((pad, IDX_DIM), F32)], axis=0).astype(BF16)
    w = jnp.concatenate([iw_ref[0], jnp.zeros((pad, 1), F32)], axis=0)
    for i in range(g):
        keys = page_refs[i][0, 0].astype(BF16)
        d = jnp.dot(iq, keys, preferred_element_type=F32)
        page = keys.shape[1]
        o_ref[0, :, i * page:(i + 1) * page] = jnp.sum(w * jnp.maximum(d, 0.0), axis=0, keepdims=True)


def _idx_sample(page_table, iq, iw, cache_ik, g):
    nb, n_pages = page_table.shape
    page = cache_ik.shape[2]
    cache_t = jnp.swapaxes(cache_ik, 2, 3)

    def page_spec(i):
        return pl.BlockSpec((1, 1, IDX_DIM, page), lambda b, j, pt: (0, pt[b, j * g + i], 0, 0))

    return pl.pallas_call(
        functools.partial(_idx_sample_kernel, g=g),
        grid_spec=pltpu.PrefetchScalarGridSpec(
            num_scalar_prefetch=1,
            grid=(nb, n_pages // g),
            in_specs=[pl.BlockSpec((1, IDX_HEADS, IDX_DIM), lambda b, j, pt: (b, 0, 0)),
                      pl.BlockSpec((1, IDX_HEADS, 1), lambda b, j, pt: (b, 0, 0))]
                     + [page_spec(i) for i in range(g)],
            out_specs=pl.BlockSpec((1, 1, g * page), lambda b, j, pt: (b, 0, j)),
        ),
        out_shape=jax.ShapeDtypeStruct((nb, 1, n_pages * page), F32),
        compiler_params=pltpu.CompilerParams(dimension_semantics=("arbitrary", "arbitrary"),
                                             vmem_limit_bytes=VMEM_LIMIT),
        name="idx_sample",
    )(page_table, iq.reshape(nb, IDX_HEADS, IDX_DIM), iw.reshape(nb, IDX_HEADS, 1), *([cache_t] * g))


def _select_sample_kernel(isc_ref, iq_ref, ikk_ref, iw_ref, bias_ref, new_ref, key_scr, extra_scr,
                          *, tk, sub, ksel, idx_bits):
    rows, past = isc_ref.shape
    nt = past // tk
    lane_t = lax.broadcasted_iota(I32, (rows, tk), 1)
    for j in range(nt):
        key_scr[j] = _score_key(isc_ref[:, j * tk:(j + 1) * tk], lane_t + j * tk)
    lane = lax.broadcasted_iota(I32, (rows, HEAD_W), 1)
    kk = ikk_ref[...].astype(F32)
    new = jnp.zeros((rows, 1), F32)
    for h in range(IDX_HEADS):
        pair = iq_ref[:, (h // 2) * HEAD_W:(h // 2 + 1) * HEAD_W].astype(F32)
        keep = (lane < IDX_DIM) if h % 2 == 0 else (lane >= IDX_DIM)
        d = jnp.sum(jnp.where(keep, pair * kk, 0.0), axis=1, keepdims=True)
        new = new + iw_ref[:, h:h + 1] * jnp.maximum(d, 0.0)
    extra_scr[...] = jnp.broadcast_to(_score_key(new, past), extra_scr.shape)

    thr_rows = _select_rows(key_scr, extra_scr, nt, 1, rows, sub, ksel, idx_bits)
    thr = jnp.tile(thr_rows, (1, tk // 128))
    for j in range(nt):
        bias_ref[:, j * tk:(j + 1) * tk] = jnp.where(key_scr[j] >= thr, 0.0, NEG_BIG)
    new_ref[...] = jnp.where(extra_scr[...] >= thr_rows, 0.0, NEG_BIG)


def _select_sample(isc, iq, ikk, iw, tk):
    rows, past = isc.shape
    ksel = min(TOPK_MAX, (past + 1) // 4)
    idx_bits = int(past + 1).bit_length()
    full = lambda a: pl.BlockSpec(a.shape, lambda i: (0,) * a.ndim)
    return pl.pallas_call(
        functools.partial(_select_sample_kernel, tk=tk, sub=min(SUB_ROWS, rows), ksel=ksel, idx_bits=idx_bits),
        grid=(1,),
        in_specs=[full(isc), full(iq), full(ikk), full(iw)],
        out_specs=[pl.BlockSpec((rows, past), lambda i: (0, 0)), pl.BlockSpec((rows, 128), lambda i: (0, 0))],
        out_shape=[jax.ShapeDtypeStruct((rows, past), F32), jax.ShapeDtypeStruct((rows, 128), F32)],
        scratch_shapes=[pltpu.VMEM((past // tk, rows, tk), I32), pltpu.VMEM((rows, 128), I32)],
        compiler_params=pltpu.CompilerParams(dimension_semantics=("arbitrary",), vmem_limit_bytes=VMEM_LIMIT),
        name="select_sample",
    )(isc, iq, ikk, iw)


def _attn_sample_kernel(pt_ref, lam_ref, subln_ref, dq_ref, dkn_ref, dvn_ref, sq_ref, skn_ref, svn_ref,
                        bias_ref, new_ref, *refs, g, page):
    dk_refs, dv_refs, sk_refs, sv_refs = refs[:g], refs[g:2 * g], refs[2 * g:3 * g], refs[3 * g:4 * g]
    od_ref, os_ref, md_scr, ld_scr, accd_scr, ms_scr, ls_scr, accs_scr = refs[4 * g:]
    j = pl.program_id(1)
    nj = pl.num_programs(1)
    gp = g * page
    past = nj * gp
    heads = DIFF_HEADS
    row = lax.broadcasted_iota(I32, (QROWS, HEAD_W), 0)
    lane = lax.broadcasted_iota(I32, (QROWS, HEAD_W), 1)
    keep_d = ((row == 0) & (lane < DIFF_QK)) | ((row == 1) & (lane >= DIFF_QK))
    keep_s = row == 0
    kpos = lax.broadcasted_iota(I32, (QROWS, gp), 1) + j * gp
    dist = (past - kpos).astype(F32)

    def head_q(q_ref, h, keep):
        return jnp.where(keep, q_ref[0][:, h * HEAD_W:(h + 1) * HEAD_W].astype(F32), 0.0).astype(BF16)

    def head_rows(refs_, h):
        return jnp.concatenate([r[0, 0, pl.ds(h, page, stride=heads), :] for r in refs_], axis=0).astype(BF16)

    qds = [head_q(dq_ref, h, keep_d) for h in range(heads)]
    qss = [head_q(sq_ref, h, keep_s) for h in range(heads)]

    @pl.when(j == 0)
    def _():
        for h in range(heads):
            hs = slice(h * HEAD_W, (h + 1) * HEAD_W)
            sd = jnp.sum(qds[h].astype(F32) * dkn_ref[0][:, hs].astype(F32), axis=1, keepdims=True)
            md_scr[h] = jnp.broadcast_to(sd, (QROWS, HEAD_W))
            ld_scr[h] = jnp.ones((QROWS, HEAD_W), F32)
            accd_scr[h] = jnp.broadcast_to(dvn_ref[0][:, hs].astype(F32), (QROWS, HEAD_W))
            ss = jnp.sum(qss[h].astype(F32) * skn_ref[0][:, hs].astype(F32), axis=1, keepdims=True)
            ms_scr[h] = jnp.broadcast_to(ss + new_ref[0][:, :1], (QROWS, HEAD_W))
            ls_scr[h] = jnp.ones((QROWS, HEAD_W), F32)
            accs_scr[h] = jnp.broadcast_to(svn_ref[0][:, hs].astype(F32), (QROWS, HEAD_W))

    selbias = bias_ref[0]
    jobs = []
    for h in range(heads):
        jobs.append((h, qds[h], dk_refs, dv_refs, -(SLOPES_DIFF[h] * LOG2E) * dist, md_scr, ld_scr, accd_scr))
        jobs.append((h, qss[h], sk_refs, sv_refs, -(SLOPES_DSA[h] * LOG2E) * dist + selbias, ms_scr, ls_scr, accs_scr))
    scores = [_nt_dot(q, head_rows(k_refs, h)) + bias for h, q, k_refs, _, bias, _, _, _ in jobs]
    probs = []
    for (h, _, _, _, _, m_scr, l_scr, acc_scr), s in zip(jobs, scores):
        m_prev = m_scr[h]
        m_new = jnp.maximum(m_prev, jnp.max(s, axis=1, keepdims=True))
        p = jnp.exp2(s - m_new[:, :1])
        alpha = jnp.exp2(m_prev - m_new)
        l_scr[h] = alpha * l_scr[h] + jnp.sum(p, axis=1, keepdims=True)
        m_scr[h] = m_new
        probs.append((p.astype(BF16), alpha))
    for (h, _, _, v_refs, _, _, _, acc_scr), (p, alpha) in zip(jobs, probs):
        acc_scr[h] = alpha * acc_scr[h] + jnp.dot(p, head_rows(v_refs, h), preferred_element_type=F32)

    @pl.when(j == nj - 1)
    def _():
        lam = _diff_lambda(lam_ref[...])
        for h in range(heads):
            hs = slice(h * HEAD_W, (h + 1) * HEAD_W)
            od = accd_scr[h] / ld_scr[h]
            od_ref[0, :, hs] = _rms(od[0:1] - lam * od[1:2], subln_ref[...]) * (1.0 - LAM_INIT)
            os_ref[0, :, hs] = (accs_scr[h] / ls_scr[h])[0:1]


def _attn_sample(page_table, lam_p, subln, dq, dk_new, dv_new, sq, sk_new, sv_new, selbias, selnew,
                 cache_dk, cache_dv, cache_sk, cache_sv, g):
    nb, n_pages = page_table.shape
    n_pool, page, heads = cache_dk.shape[1:4]
    assert heads == DIFF_HEADS == DSA_HEADS
    gp = g * page
    flat = lambda c: c.reshape(c.shape[0], n_pool, page * heads, HEAD_W)
    vec = lambda a: a.reshape(nb, 1, a.shape[-1])

    def page_spec(i):
        return pl.BlockSpec((1, 1, page * heads, HEAD_W), lambda b, j, pt: (0, pt[b, j * g + i], 0, 0))

    per_b = lambda w: pl.BlockSpec((1, 1, w), lambda b, j, pt: (b, 0, 0))
    const = lambda a: pl.BlockSpec(a.shape, lambda b, j, pt: (0,) * a.ndim)
    pages = [page_spec(i) for i in range(g)]
    state = [pltpu.VMEM((heads, QROWS, HEAD_W), F32)] * 3
    return pl.pallas_call(
        functools.partial(_attn_sample_kernel, g=g, page=page),
        grid_spec=pltpu.PrefetchScalarGridSpec(
            num_scalar_prefetch=1,
            grid=(nb, n_pages // g),
            in_specs=[const(lam_p), const(subln)] + [per_b(WIDTH)] * 6
                     + [pl.BlockSpec((1, 1, gp), lambda b, j, pt: (b, 0, j)), per_b(128)] + pages * 4,
            out_specs=[per_b(WIDTH), per_b(WIDTH)],
            scratch_shapes=state * 2,
        ),
        out_shape=[jax.ShapeDtypeStruct((nb, 1, WIDTH), F32)] * 2,
        compiler_params=pltpu.CompilerParams(dimension_semantics=("arbitrary", "arbitrary"),
                                             vmem_limit_bytes=VMEM_LIMIT),
        name="attn_sample",
    )(page_table, lam_p, subln, vec(dq), vec(dk_new), vec(dv_new), vec(sq), vec(sk_new), vec(sv_new),
      selbias.reshape(nb, 1, -1), vec(selnew),
      *([flat(cache_dk)] * g), *([flat(cache_dv)] * g), *([flat(cache_sk)] * g), *([flat(cache_sv)] * g))


def _silu(x):
    return x / (1.0 + jnp.exp(-x))


def _residual_kernel(h_ref, od_ref, os_ref, dg_ref, sg_ref, p_ref, wo_ref, gp_ref, wg_ref, wp_ref, gf_ref, y_ref):
    a = (od_ref[...] * _silu(dg_ref[...])).astype(BF16)
    b = (os_ref[...] * _silu(sg_ref[...])).astype(BF16)
    h = h_ref[...] + jnp.dot(a, wo_ref[:WIDTH, :], preferred_element_type=F32) \
        + jnp.dot(b, wo_ref[WIDTH:, :], preferred_element_type=F32)
    gate_in = _rms(h, gp_ref[...]).astype(BF16)
    gate = 1.0 / (1.0 + jnp.exp(-jnp.dot(gate_in, wg_ref[...], preferred_element_type=F32)))
    h = h + gate * jnp.dot(p_ref[...].astype(BF16), wp_ref[...], preferred_element_type=F32)
    y_ref[...] = _rms(h, gf_ref[...])


def _residual(h, od, osp, dg, sg, p, w_out, g_ple, w_gate, w_proj, g_final, tm):
    m, d = h.shape
    rows = lambda w: pl.BlockSpec((tm, w), lambda i: (i, 0))
    const = lambda a: pl.BlockSpec(a.shape, lambda i: (0,) * a.ndim)
    return pl.pallas_call(
        _residual_kernel,
        grid=(m // tm,),
        in_specs=[rows(d), rows(WIDTH), rows(WIDTH), rows(WIDTH), rows(WIDTH), rows(p.shape[1]),
                  const(w_out), const(g_ple), const(w_gate), const(w_proj), const(g_final)],
        out_specs=rows(d),
        out_shape=jax.ShapeDtypeStruct((m, d), F32),
        compiler_params=pltpu.CompilerParams(dimension_semantics=("arbitrary",), vmem_limit_bytes=VMEM_LIMIT),
        name="residual",
    )(h, od, osp, dg, sg, p, w_out, g_ple, w_gate, w_proj, g_final)


def _row_tile(m, want):
    t = min(m, want)
    assert m % t == 0
    return t


def kernel(x_prompt, x_sample, cache_diff_k, cache_diff_v, cache_dsa_k, cache_dsa_v, cache_idx_k, page_table,
           p_prompt, p_sample, norm_mix, w_in, diff_lambda, diff_subln, w_out, norm_ple, w_ple_gate, w_ple_proj,
           norm_final):
    batch, seq, d_model = x_prompt.shape
    nb = x_sample.shape[0]
    assert norm_mix.shape[0] == 1 and x_sample.shape[1] == 1
    n_main = N_SEG * SEG
    n_tail = IDX_DIM + IDX_HEADS
    assert w_in.shape[2] == n_main + n_tail

    w = w_in[0]
    w_main = w[:, :n_main].astype(BF16)
    ik_cols = w[:, n_main:n_main + IDX_DIM]
    w_tail = jnp.concatenate(
        [ik_cols, ik_cols, w[:, n_main + IDX_DIM:], jnp.zeros((d_model, TAIL_W - 2 * IDX_DIM - IDX_HEADS), F32)],
        axis=1).astype(BF16)
    g_mix = norm_mix[0][None]
    subln = diff_subln[0][None]
    lam_p = diff_lambda[0]
    res_w = (w_out[0].astype(BF16), norm_ple[0][None], w_ple_gate[0].astype(BF16), w_ple_proj[0].astype(BF16),
             norm_final[None])

    mp = batch * seq
    xp = x_prompt.reshape(mp, d_model)
    (dq, dk32, dk16, dv32, dv16, dg, sq, sk32, sk16, sv32, sv16, sg, iq, ik32, ikk, iw) = _project(
        xp, g_mix, w_main, w_tail, _row_tile(mp, PROJ_ROWS))
    t = _row_tile(seq, Q_TILE)
    assert seq % BIG == 0 and BIG % t == 0
    od = _diff_prompt(lam_p, subln, dq, dk16, dv16, batch, seq, t)
    osp = _dsa_prompt(sq, sk16, sv16, iq, ikk, iw, batch, seq, t)
    y_prompt = _residual(xp, od, osp, dg, sg, p_prompt[0].reshape(mp, -1), *res_w, _row_tile(mp, PROJ_ROWS))

    xs = x_sample.reshape(nb, d_model)
    (sdq, sdk32, sdk16, sdv32, sdv16, sdg, ssq, ssk32, ssk16, ssv32, ssv16, ssg, siq, sik32, sikk, siw) = _project(
        xs, g_mix, w_main, w_tail, _row_tile(nb, PROJ_ROWS))
    n_pages = page_table.shape[1]
    isc = _idx_sample(page_table, siq, siw, cache_idx_k, math.gcd(n_pages, IDX_PAGES)).reshape(nb, -1)
    selbias, selnew = _select_sample(isc, siq, sikk, siw, _row_tile(isc.shape[1], SELECT_TILE))
    od_s, os_s = _attn_sample(page_table, lam_p, subln, sdq, sdk16, sdv16, ssq, ssk16, ssv16, selbias, selnew,
                              cache_diff_k, cache_diff_v, cache_dsa_k, cache_dsa_v, math.gcd(n_pages, ATTN_PAGES))
    y_sample = _residual(xs, od_s.reshape(nb, WIDTH), os_s.reshape(nb, WIDTH), sdg, ssg,
                         p_sample[0].reshape(nb, -1), *res_w, _row_tile(nb, PROJ_ROWS))

    ph = lambda a, hd: a.reshape(1, batch, seq, hd, -1)
    sh = lambda a, hd: a.reshape(1, nb, 1, hd, -1)
    return (y_prompt.reshape(batch, seq, d_model), y_sample.reshape(nb, 1, d_model),
            ph(dk32, DIFF_HEADS), ph(dv32, DIFF_HEADS), ph(sk32, DSA_HEADS), ph(sv32, DSA_HEADS),
            ik32.reshape(1, batch, seq, IDX_DIM),
            sh(sdk32, DIFF_HEADS), sh(sdv32, DIFF_HEADS), sh(ssk32, DSA_HEADS), sh(ssv32, DSA_HEADS),
            sik32.reshape(1, nb, 1, IDX_DIM))
```
